```python
import math
import jax
import jax.numpy as jnp
from jax import lax
import numpy as np

D_MODEL = 1024
BATCH = 2
SEQ = 16384
DEPTH = 4

N_MIXERS = 4
CHUNK = 64
CONV_W = 4
RMS_EPS = 1e-6

ML_HEADS = 4
ML_DV = D_MODEL // ML_HEADS
ML_DK = ML_DV // 2

SSD_INNER = 2 * D_MODEL
SSD_HEADDIM = 64
SSD_HEADS = SSD_INNER // SSD_HEADDIM
SSD_GROUPS = 4
SSD_HPG = SSD_HEADS // SSD_GROUPS
SSD_STATE = 128
SSD_CONV_DIM = SSD_INNER + 2 * SSD_GROUPS * SSD_STATE

RW_HEADDIM = 64
RW_HEADS = D_MODEL // RW_HEADDIM
RW_DECAY_LORA = 64
RW_A_LORA = 64
RW_GATE_LORA = 160
RW_GN_EPS = 64e-5

LRU_WIDTH = D_MODEL
LRU_BLOCKS = 4
LRU_BLOCK = LRU_WIDTH // LRU_BLOCKS
LRU_C = 8.0

D_FF = 7 * D_MODEL // 2
N_EXPERTS = 8
TOP_K = 2

N_MLSTM_LAYERS = (DEPTH + 3) // 4
N_SSD_LAYERS = (DEPTH + 2) // 4
N_RWKV_LAYERS = (DEPTH + 1) // 4
N_LRU_LAYERS = DEPTH // 4
N_DENSE_LAYERS = (DEPTH + 1) // 2
N_MOE_LAYERS = DEPTH // 2

kernel_name = 'hybrid_mlstm_ssd_rwkv7_rglru_moe_adaln'


def _split(t, sizes):
    return jnp.split(t, np.cumsum(sizes)[:-1].tolist(), axis=-1)


def _to_chunks(t):
    b, s = t.shape[:2]
    return jnp.moveaxis(t.reshape(b, s // CHUNK, CHUNK, *t.shape[2:]), 1, 0)


def _from_chunks(t):
    nc, b, l = t.shape[:3]
    return jnp.moveaxis(t, 0, 1).reshape(b, nc * l, *t.shape[3:])


def rmsnorm(x, w):
    xf = x.astype(jnp.float32)
    y = xf * lax.rsqrt(jnp.mean(xf * xf, axis=-1, keepdims=True) + RMS_EPS)
    return (y * w.astype(jnp.float32)).astype(x.dtype)


def causal_dwconv(x, w, b):
    k, ch = w.shape
    y = lax.conv_general_dilated(x, w[:, None, :], window_strides=(1,), padding=[(k - 1, 0)],
                                 dimension_numbers=('NWC', 'WIO', 'NWC'), feature_group_count=ch)
    return y + b


def _mlstm_chunk(carry, inp):
    c_mat, n_vec, m = carry
    q, k, v, li, lf = inp
    l = q.shape[1]
    g = jnp.cumsum(lf, axis=-1)
    causal = jnp.tril(jnp.ones((l, l), bool))
    log_d = jnp.where(causal, g[..., :, None] - g[..., None, :] + li[..., None, :], -jnp.inf)
    log_inter = g + m[..., None]
    m_row = jnp.maximum(jnp.max(log_d, axis=-1), log_inter)
    s = jnp.einsum('blhd,bshd->bhls', q, k) * jnp.exp(log_d - m_row[..., None])
    e_inter = jnp.exp(log_inter - m_row)
    num = jnp.einsum('bhls,bshv->blhv', s, v) + jnp.einsum('bhl,blhd,bhdv->blhv', e_inter, q, c_mat)
    den = jnp.sum(s, axis=-1) + e_inter * jnp.einsum('blhd,bhd->bhl', q, n_vec)
    denom = jnp.maximum(jnp.abs(den), jnp.exp(-m_row))
    h = num / jnp.swapaxes(denom, 1, 2)[..., None]
    g_last = g[..., -1]
    log_w = g_last[..., None] - g + li
    m_new = jnp.maximum(g_last + m, jnp.max(log_w, axis=-1))
    w_s = jnp.exp(log_w - m_new[..., None])
    decay = jnp.exp(g_last + m - m_new)
    c_mat = decay[..., None, None] * c_mat + jnp.einsum('bhs,bshd,bshv->bhdv', w_s, k, v)
    n_vec = decay[..., None] * n_vec + jnp.einsum('bhs,bshd->bhd', w_s, k)
    return (c_mat, n_vec, m_new), h


def mlstm_mixer(h, w_in, gate_b, norm_w, w_out):
    b, s, _ = h.shape
    f32 = jnp.float32
    hk, hv = ML_HEADS * ML_DK, ML_HEADS * ML_DV
    q, k, v, o, gi, gf = _split(h @ w_in, [hk, hk, hv, hv, ML_HEADS, ML_HEADS])
    q = q.reshape(b, s, ML_HEADS, ML_DK).astype(f32)
    k = k.reshape(b, s, ML_HEADS, ML_DK).astype(f32) * (ML_DK ** -0.5)
    v = v.reshape(b, s, ML_HEADS, ML_DV).astype(f32)
    log_i = jnp.swapaxes(gi.astype(f32) + gate_b[0].astype(f32), 1, 2)
    log_f = jnp.swapaxes(jax.nn.log_sigmoid(gf.astype(f32) + gate_b[1].astype(f32)), 1, 2)
    nc = s // CHUNK
    gc = lambda t: jnp.moveaxis(t.reshape(b, ML_HEADS, nc, CHUNK), 2, 0)
    init = (jnp.zeros((b, ML_HEADS, ML_DK, ML_DV), f32), jnp.zeros((b, ML_HEADS, ML_DK), f32),
            jnp.zeros((b, ML_HEADS), f32))
    _, hc = lax.scan(_mlstm_chunk, init, (_to_chunks(q), _to_chunks(k), _to_chunks(v), gc(log_i), gc(log_f)))
    hh = _from_chunks(hc)
    hh = hh * lax.rsqrt(jnp.mean(hh * hh, axis=-1, keepdims=True) + RMS_EPS)
    hh = hh.reshape(b, s, hv) * norm_w.astype(f32)
    return (hh * jax.nn.sigmoid(o.astype(f32))).astype(h.dtype) @ w_out


def _ssd_chunk(state, inp):
    xc, dtc, lac, bc, cc = inp
    l = xc.shape[1]
    acum = jnp.cumsum(lac, axis=1)
    causal = jnp.tril(jnp.ones((l, l), bool))[None, :, :, None, None]
    decay = jnp.exp(jnp.where(causal, acum[:, :, None] - acum[:, None, :], -jnp.inf))
    cb = jnp.einsum('bjgn,bsgn->bjsg', cc, bc)
    wts = cb[..., None] * decay * dtc[:, None]
    y = jnp.einsum('bjsge,bsgep->bjgep', wts, xc)
    y = y + jnp.exp(acum)[..., None] * jnp.einsum('bjgn,bgepn->bjgep', cc, state)
    a_last = acum[:, -1]
    w_s = jnp.exp(a_last[:, None] - acum) * dtc
    state = jnp.exp(a_last)[..., None, None] * state + jnp.einsum('bsge,bsgep,bsgn->bgepn', w_s, xc, bc)
    return state, y


def ssd_mixer(h, w_in, conv_w, conv_b, dt_bias, a_log, d_skip, norm_w, w_out):
    b, s, _ = h.shape
    f32 = jnp.float32
    gn = SSD_GROUPS * SSD_STATE
    z, xbc, dt = _split(h @ w_in, [SSD_INNER, SSD_CONV_DIM, SSD_HEADS])
    xbc = jax.nn.silu(causal_dwconv(xbc, conv_w, conv_b))
    xs, bm, cm = _split(xbc, [SSD_INNER, gn, gn])
    xs = xs.reshape(b, s, SSD_GROUPS, SSD_HPG, SSD_HEADDIM).astype(f32)
    bm = bm.reshape(b, s, SSD_GROUPS, SSD_STATE).astype(f32)
    cm = cm.reshape(b, s, SSD_GROUPS, SSD_STATE).astype(f32)
    dt = jax.nn.softplus(dt.astype(f32) + dt_bias.astype(f32)).reshape(b, s, SSD_GROUPS, SSD_HPG)
    log_a = dt * (-jnp.exp(a_log.astype(f32))).reshape(SSD_GROUPS, SSD_HPG)
    init = jnp.zeros((b, SSD_GROUPS, SSD_HPG, SSD_HEADDIM, SSD_STATE), f32)
    _, yc = lax.scan(_ssd_chunk, init, (_to_chunks(xs), _to_chunks(dt), _to_chunks(log_a),
                                        _to_chunks(bm), _to_chunks(cm)))
    y = _from_chunks(yc) + d_skip.astype(f32).reshape(SSD_GROUPS, SSD_HPG, 1) * xs
    y = (y.reshape(b, s, SSD_INNER) * jax.nn.silu(z.astype(f32))).reshape(b, s, SSD_GROUPS, -1)
    y = y * lax.rsqrt(jnp.mean(y * y, axis=-1, keepdims=True) + RMS_EPS)
    y = y.reshape(b, s, SSD_INNER) * norm_w.astype(f32)
    return y.astype(h.dtype) @ w_out


def _rwkv7_step(state, inp):
    r, w, k, v, kk, a = inp
    sa = jnp.einsum('bhvk,bhk->bhv', state, -kk)
    state = (state * w[:, :, None, :] + sa[..., None] * (kk * a)[:, :, None, :]
             + v[..., None] * k[:, :, None, :])
    return state, jnp.einsum('bhvk,bhk->bhv', state, r)


def rwkv7_mixer(h, w_in, mu, w0, w_lora_b, a0, a_lora_b, g_lora_b, k_k, k_a, r_k, ln_w, ln_b, w_out):
    b, s, d = h.shape
    f32 = jnp.float32
    sizes = [d, d, d, RW_DECAY_LORA, RW_A_LORA, RW_GATE_LORA]
    mu_cols = jnp.concatenate([jnp.broadcast_to(mu[i][:, None], (d, n)) for i, n in enumerate(sizes)], axis=1)
    xx = jnp.pad(h, ((0, 0), (1, 0), (0, 0)))[:, :-1] - h
    proj = h @ w_in + xx @ (w_in * mu_cols)
    r, k, v, dw, da, dg = _split(proj, sizes)
    w_log = -jax.nn.softplus(-(w0 + jnp.tanh(dw) @ w_lora_b).astype(f32)) - 0.5
    decay = jnp.exp(-jnp.exp(w_log))
    a = jax.nn.sigmoid((a0 + da @ a_lora_b).astype(f32))
    g = jax.nn.sigmoid(dg) @ g_lora_b
    heads = lambda t: t.astype(f32).reshape(b, s, RW_HEADS, RW_HEADDIM)
    r, k, v, decay, a = heads(r), heads(k), heads(v), heads(decay), heads(a)
    kk = k * k_k.astype(f32).reshape(RW_HEADS, RW_HEADDIM)
    kk = kk / jnp.maximum(jnp.linalg.norm(kk, axis=-1, keepdims=True), 1e-12)
    k = k * (1 + (a - 1) * k_a.astype(f32).reshape(RW_HEADS, RW_HEADDIM))
    tm = lambda t: jnp.moveaxis(t, 1, 0)
    init = jnp.zeros((b, RW_HEADS, RW_HEADDIM, RW_HEADDIM), f32)
    _, y = lax.scan(_rwkv7_step, init, (tm(r), tm(decay), tm(k), tm(v), tm(kk), tm(a)))
    y = jnp.moveaxis(y, 0, 1)
    mean = jnp.mean(y, axis=-1, keepdims=True)
    var = jnp.mean(jnp.square(y - mean), axis=-1, keepdims=True)
    y = ((y - mean) * lax.rsqrt(var + RW_GN_EPS)).reshape(b, s, d) * ln_w.astype(f32) + ln_b.astype(f32)
    bonus = jnp.sum(r * k * r_k.astype(f32), axis=-1, keepdims=True) * v
    y = y + bonus.reshape(b, s, d)
    return (y * g.astype(f32)).astype(h.dtype) @ w_out


def _linear_combine(c1, c2):
    a1, b1 = c1
    a2, b2 = c2
    return a1 * a2, a2 * b1 + b2


def rglru_mixer(h, w_in, conv_w, conv_b, gx_w, gx_b, ga_w, ga_b, lam, w_out):
    b, s, _ = h.shape
    f32 = jnp.float32
    y_br, x_br = _split(h @ w_in, [LRU_WIDTH, LRU_WIDTH])
    xb = causal_dwconv(x_br, conv_w, conv_b).astype(f32).reshape(b, s, LRU_BLOCKS, LRU_BLOCK)
    gate_x = jax.nn.sigmoid(jnp.einsum('bsnd,nde->bsne', xb, gx_w.astype(f32)) + gx_b.astype(f32))
    gate_a = jax.nn.sigmoid(jnp.einsum('bsnd,nde->bsne', xb, ga_w.astype(f32)) + ga_b.astype(f32))
    log_a = LRU_C * gate_a * jax.nn.log_sigmoid(lam.astype(f32)).reshape(LRU_BLOCKS, LRU_BLOCK)
    u = jnp.sqrt(-jnp.expm1(2.0 * log_a)) * gate_x * xb
    _, hs = lax.associative_scan(_linear_combine, (jnp.exp(log_a), u), axis=1)
    out = jax.nn.gelu(y_br.astype(f32)) * hs.reshape(b, s, LRU_WIDTH)
    return out.astype(h.dtype) @ w_out


def swiglu(h, w_gu, w_down):
    gate, up = jnp.split(h @ w_gu, 2, axis=-1)
    return (jax.nn.silu(gate) * up) @ w_down


def moe_ffn(h, w_router, w_gu, w_down):
    logits = (h @ w_router).astype(jnp.float32)
    top_logits, top_idx = lax.top_k(logits, TOP_K)
    top_w = jax.nn.softmax(top_logits, axis=-1)
    combine = jnp.einsum('bske,bsk->bse', jax.nn.one_hot(top_idx, N_EXPERTS, dtype=jnp.float32),
                         top_w).astype(h.dtype)
    out = jnp.zeros_like(h)
    for e in range(N_EXPERTS):
        out = out + combine[..., e:e + 1] * swiglu(h, w_gu[e], w_down[e])
    return out


def setup_inputs(seed: int = 0) -> dict:
    key = jax.random.key(seed)
    ks = iter(jax.random.split(key, 96))
    f32 = jnp.float32
    d = D_MODEL

    def nrm(shape, scale):
        return jax.random.normal(next(ks), shape, f32) * scale

    def unif(shape, lo, hi):
        return jax.random.uniform(next(ks), shape, f32, lo, hi)

    def gain(shape):
        return 1.0 + nrm(shape, 0.01)

    nm, nss, nrw, nlr = N_MLSTM_LAYERS, N_SSD_LAYERS, N_RWKV_LAYERS, N_LRU_LAYERS
    nd, ne = N_DENSE_LAYERS, N_MOE_LAYERS
    ml_cols = 2 * ML_HEADS * ML_DK + 2 * ML_HEADS * ML_DV + 2 * ML_HEADS
    ssd_cols = 2 * SSD_INNER + 2 * SSD_GROUPS * SSD_STATE + SSD_HEADS
    rw_cols = 3 * d + RW_DECAY_LORA + RW_A_LORA + RW_GATE_LORA

    ml_gate_b = jnp.stack([nrm((nm, ML_HEADS), 0.1),
                           jnp.linspace(3.0, 6.0, ML_HEADS) + nrm((nm, ML_HEADS), 0.1)], axis=1)
    dt0 = jnp.exp(unif((nss, SSD_HEADS), math.log(1e-3), math.log(1e-1)))
    p = unif((nlr, LRU_WIDTH), 0.9, 0.999) ** (1.0 / LRU_C)

    return {
        'x': nrm((BATCH, SEQ, d), 1.0),
        'c': nrm((BATCH, d), 1.0),
        'ada_w': nrm((DEPTH, d, 6 * d), 0.5 * d ** -0.5),
        'ada_b': nrm((DEPTH, 6 * d), 0.02),
        'norm_w': gain((DEPTH, 2, d)),
        'final_norm_w': gain((d,)),
        'ml_w_in': nrm((nm, d, ml_cols), d ** -0.5),
        'ml_gate_b': ml_gate_b,
        'ml_norm_w': gain((nm, ML_HEADS * ML_DV)),
        'ml_w_out': nrm((nm, ML_HEADS * ML_DV, d), (ML_HEADS * ML_DV) ** -0.5),
        'ssd_w_in': nrm((nss, d, ssd_cols), d ** -0.5),
        'ssd_conv_w': nrm((nss, CONV_W, SSD_CONV_DIM), CONV_W ** -0.5),
        'ssd_conv_b': nrm((nss, SSD_CONV_DIM), 0.02),
        'ssd_dt_bias': dt0 + jnp.log(-jnp.expm1(-dt0)),
        'ssd_a_log': jnp.log(unif((nss, SSD_HEADS), 1.0, 16.0)),
        'ssd_d': gain((nss, SSD_HEADS)),
        'ssd_norm_w': gain((nss, SSD_INNER)),
        'ssd_w_out': nrm((nss, SSD_INNER, d), SSD_INNER ** -0.5),
        'rw_w_in': nrm((nrw, d, rw_cols), d ** -0.5),
        'rw_mu': unif((nrw, 6, d), 0.0, 1.0),
        'rw_w0': unif((nrw, d), -6.0, -1.0),
        'rw_w_lora_b': nrm((nrw, RW_DECAY_LORA, d), 0.1 * RW_DECAY_LORA ** -0.5),
        'rw_a0': nrm((nrw, d), 0.1),
        'rw_a_lora_b': nrm((nrw, RW_A_LORA, d), 0.5 * RW_A_LORA ** -0.5),
        'rw_g_lora_b': nrm((nrw, RW_GATE_LORA, d), RW_GATE_LORA ** -0.5),
        'rw_k_k': 0.85 + nrm((nrw, d), 0.01),
        'rw_k_a': gain((nrw, d)),
        'rw_r_k': nrm((nrw, RW_HEADS, RW_HEADDIM), 0.1),
        'rw_ln_w': gain((nrw, d)),
        'rw_ln_b': nrm((nrw, d), 0.02),
        'rw_w_out': nrm((nrw, d, d), d ** -0.5),
        'lru_w_in': nrm((nlr, d, 2 * LRU_WIDTH), d ** -0.5),
        'lru_conv_w': nrm((nlr, CONV_W, LRU_WIDTH), CONV_W ** -0.5),
        'lru_conv_b': nrm((nlr, LRU_WIDTH), 0.02),
        'lru_gx_w': nrm((nlr, LRU_BLOCKS, LRU_BLOCK, LRU_BLOCK), LRU_BLOCK ** -0.5),
        'lru_gx_b': nrm((nlr, LRU_BLOCKS, LRU_BLOCK), 0.02),
        'lru_ga_w': nrm((nlr, LRU_BLOCKS, LRU_BLOCK, LRU_BLOCK), LRU_BLOCK ** -0.5),
        'lru_ga_b': nrm((nlr, LRU_BLOCKS, LRU_BLOCK), 0.02),
        'lru_lambda': jnp.log(p) - jnp.log1p(-p),
        'lru_w_out': nrm((nlr, LRU_WIDTH, d), LRU_WIDTH ** -0.5),
        'ffn_w_gu': nrm((nd, d, 2 * D_FF), d ** -0.5),
        'ffn_w_down': nrm((nd, D_FF, d), D_FF ** -0.5),
        'moe_router': nrm((ne, d, N_EXPERTS), d ** -0.5),
        'moe_w_gu': nrm((ne, N_EXPERTS, d, 2 * D_FF), d ** -0.5),
        'moe_w_down': nrm((ne, N_EXPERTS, D_FF, d), D_FF ** -0.5),
    }


def reference(x, c, ada_w, ada_b, norm_w, final_norm_w,
              ml_w_in, ml_gate_b, ml_norm_w, ml_w_out,
              ssd_w_in, ssd_conv_w, ssd_conv_b, ssd_dt_bias, ssd_a_log, ssd_d, ssd_norm_w, ssd_w_out,
              rw_w_in, rw_mu, rw_w0, rw_w_lora_b, rw_a0, rw_a_lora_b, rw_g_lora_b, rw_k_k, rw_k_a,
              rw_r_k, rw_ln_w, rw_ln_b, rw_w_out,
              lru_w_in, lru_conv_w, lru_conv_b, lru_gx_w, lru_gx_b, lru_ga_w, lru_ga_b, lru_lambda,
              lru_w_out,
              ffn_w_gu, ffn_w_down, moe_router, moe_w_gu, moe_w_down):
    cond = jax.nn.silu(c)
    for layer in range(DEPTH):
        mod = (cond @ ada_w[layer] + ada_b[layer])[:, None, :]
        sh_t, sc_t, g_t, sh_c, sc_c, g_c = jnp.split(mod, 6, axis=-1)
        h = rmsnorm(x, norm_w[layer, 0]) * (1 + sc_t) + sh_t
        kind, j = layer % N_MIXERS, layer // N_MIXERS
        if kind == 0:
            y = mlstm_mixer(h, ml_w_in[j], ml_gate_b[j], ml_norm_w[j], ml_w_out[j])
        elif kind == 1:
            y = ssd_mixer(h, ssd_w_in[j], ssd_conv_w[j], ssd_conv_b[j], ssd_dt_bias[j], ssd_a_log[j],
                          ssd_d[j], ssd_norm_w[j], ssd_w_out[j])
        elif kind == 2:
            y = rwkv7_mixer(h, rw_w_in[j], rw_mu[j], rw_w0[j], rw_w_lora_b[j], rw_a0[j], rw_a_lora_b[j],
                            rw_g_lora_b[j], rw_k_k[j], rw_k_a[j], rw_r_k[j], rw_ln_w[j], rw_ln_b[j],
                            rw_w_out[j])
        else:
            y = rglru_mixer(h, lru_w_in[j], lru_conv_w[j], lru_conv_b[j], lru_gx_w[j], lru_gx_b[j],
                            lru_ga_w[j], lru_ga_b[j], lru_lambda[j], lru_w_out[j])
        x = x + g_t * y
        h = rmsnorm(x, norm_w[layer, 1]) * (1 + sc_c) + sh_c
        if layer % 2 == 0:
            y = swiglu(h, ffn_w_gu[layer // 2], ffn_w_down[layer // 2])
        else:
            y = moe_ffn(h, moe_router[layer // 2], moe_w_gu[layer // 2], moe_w_down[layer // 2])
        x = x + g_c * y
    return rmsnorm(x, final_norm_w)
```

```python
import functools
import math

import jax
import jax.numpy as jnp
from jax import lax
from jax.experimental import pallas as pl
from jax.experimental.pallas import tpu as pltpu

F32 = jnp.float32
BF16 = jnp.bfloat16

RMS_EPS = 1e-6
LANES = 128
VMEM_LIMIT_BYTES = 56 * 1024 * 1024


def _cparams(*sem):
    return pltpu.CompilerParams(dimension_semantics=sem, vmem_limit_bytes=VMEM_LIMIT_BYTES)


def _sigmoid(x):
    return 1.0 / (1.0 + jnp.exp(-x))


def _softplus(x):
    return jnp.maximum(x, 0.0) + jnp.log(1.0 + jnp.exp(-jnp.abs(x)))


def _bdot(a, b):
    return jnp.dot(a.astype(BF16), b.astype(BF16), preferred_element_type=F32)


def _split_bf16(a):
    hi = a.astype(BF16)
    lo = (a - hi.astype(F32)).astype(BF16)
    return hi, lo


def _dot3(a, b):
    a_hi, a_lo = _split_bf16(a)
    b_hi, b_lo = _split_bf16(b)
    d = functools.partial(jnp.dot, preferred_element_type=F32)
    return d(a_hi, b_hi) + d(a_lo, b_hi) + d(a_hi, b_lo)


def _norm_mod(x, nw, sc, sh):
    ms = jnp.mean(x * x, axis=-1, keepdims=True)
    y = x * lax.rsqrt(ms + RMS_EPS) * nw
    return y * (1.0 + sc) + sh


def _pad_cols(w, n):
    return jnp.pad(w, ((0, 0), (0, n - w.shape[1])))


def _ada_kernel(c_ref, w_ref, b_ref, o_ref):
    c = c_ref[...]
    cond = c * _sigmoid(c)
    o_ref[...] = _dot3(cond, w_ref[...]) + b_ref[...]


def _ada_modulation(c, ada_w, ada_b):
    depth, d, n = ada_w.shape
    b = c.shape[0]
    rows = 8
    cp = jnp.pad(c, ((0, rows - b), (0, 0)))
    tn = 1024
    out = pl.pallas_call(
        _ada_kernel,
        grid=(depth, n // tn),
        in_specs=[
            pl.BlockSpec((rows, d), lambda l, j: (0, 0)),
            pl.BlockSpec((None, d, tn), lambda l, j: (l, 0, j)),
            pl.BlockSpec((None, 1, tn), lambda l, j: (l, 0, j)),
        ],
        out_specs=pl.BlockSpec((None, rows, tn), lambda l, j: (l, 0, j)),
        out_shape=jax.ShapeDtypeStruct((depth, rows, n), F32),
        compiler_params=_cparams("parallel", "parallel"),
        name="ada_modulation",
    )(cp, ada_w, ada_b.reshape(depth, 1, n))
    return out[:, :b]


def _nmm_kernel(x_ref, nw_ref, sc_ref, sh_ref, w_ref, *o_refs, segs):
    hb = _norm_mod(x_ref[...], nw_ref[...], sc_ref[...], sh_ref[...]).astype(BF16)
    off = 0
    for o_ref, n in zip(o_refs, segs):
        o_ref[...] = jnp.dot(hb, w_ref[:, off:off + n],
                             preferred_element_type=F32).astype(o_ref.dtype)
        off += n


def _norm_mod_matmul(x, nw, sc, sh, w_bf16, segs, dtypes, tm=512):
    b, s, d = x.shape
    ntot = w_bf16.shape[1]
    assert sum(segs) == ntot and s % tm == 0
    vec = pl.BlockSpec((None, 1, d), lambda bi, i: (bi, 0, 0))
    return pl.pallas_call(
        functools.partial(_nmm_kernel, segs=tuple(segs)),
        grid=(b, s // tm),
        in_specs=[
            pl.BlockSpec((None, tm, d), lambda bi, i: (bi, i, 0)),
            pl.BlockSpec((1, d), lambda bi, i: (0, 0)),
            vec, vec,
            pl.BlockSpec((d, ntot), lambda bi, i: (0, 0)),
        ],
        out_specs=[pl.BlockSpec((None, tm, n), lambda bi, i: (bi, i, 0)) for n in segs],
        out_shape=[jax.ShapeDtypeStruct((b, s, n), dt) for n, dt in zip(segs, dtypes)],
        compiler_params=_cparams("parallel", "parallel"),
        name="norm_mod_matmul",
    )(x, nw.reshape(1, d), sc, sh, w_bf16)


def _outproj_kernel(x_ref, y_ref, g_ref, w_ref, o_ref):
    o_ref[...] = x_ref[...] + g_ref[...] * jnp.dot(
        y_ref[...], w_ref[...], preferred_element_type=F32)


def _outproj_residual(x, y, g, w_bf16, tm=512):
    b, s, d = x.shape
    k = y.shape[-1]
    return pl.pallas_call(
        _outproj_kernel,
        grid=(b, s // tm),
        in_specs=[
            pl.BlockSpec((None, tm, d), lambda bi, i: (bi, i, 0)),
            pl.BlockSpec((None, tm, k), lambda bi, i: (bi, i, 0)),
            pl.BlockSpec((None, 1, d), lambda bi, i: (bi, 0, 0)),
            pl.BlockSpec((k, d), lambda bi, i: (0, 0)),
        ],
        out_specs=pl.BlockSpec((None, tm, d), lambda bi, i: (bi, i, 0)),
        out_shape=jax.ShapeDtypeStruct((b, s, d), F32),
        input_output_aliases={0: 0},
        compiler_params=_cparams("parallel", "parallel"),
        name="outproj_residual",
    )(x, y, g, w_bf16)


def _ffn_kernel(x_ref, nw_ref, sc_ref, sh_ref, g_ref, wg_ref, wu_ref, wd_ref, o_ref,
                h_scr, acc_scr):
    f = pl.program_id(2)

    @pl.when(f == 0)
    def _():
        h_scr[...] = _norm_mod(x_ref[...], nw_ref[...], sc_ref[...], sh_ref[...]).astype(BF16)
        acc_scr[...] = jnp.zeros_like(acc_scr)

    hb = h_scr[...]
    gate = jnp.dot(hb, wg_ref[...], preferred_element_type=F32)
    up = jnp.dot(hb, wu_ref[...], preferred_element_type=F32)
    act = (gate * _sigmoid(gate) * up).astype(BF16)
    acc_scr[...] += jnp.dot(act, wd_ref[...], preferred_element_type=F32)

    @pl.when(f == pl.num_programs(2) - 1)
    def _():
        o_ref[...] = x_ref[...] + g_ref[...] * acc_scr[...]


def _ffn_dense(x, nw, sc, sh, g, w_gu_bf16, w_down_bf16, tm=1024, tf=512):
    b, s, d = x.shape
    dff = w_down_bf16.shape[0]
    nf = dff // tf
    vec = pl.BlockSpec((None, 1, d), lambda bi, i, f: (bi, 0, 0))
    return pl.pallas_call(
        _ffn_kernel,
        grid=(b, s // tm, nf),
        in_specs=[
            pl.BlockSpec((None, tm, d), lambda bi, i, f: (bi, i, 0)),
            pl.BlockSpec((1, d), lambda bi, i, f: (0, 0)),
            vec, vec, vec,
            pl.BlockSpec((d, tf), lambda bi, i, f: (0, f)),
            pl.BlockSpec((d, tf), lambda bi, i, f: (0, f + nf)),
            pl.BlockSpec((tf, d), lambda bi, i, f: (f, 0)),
        ],
        out_specs=pl.BlockSpec((None, tm, d), lambda bi, i, f: (bi, i, 0)),
        out_shape=jax.ShapeDtypeStruct((b, s, d), F32),
        scratch_shapes=[pltpu.VMEM((tm, d), BF16), pltpu.VMEM((tm, d), F32)],
        input_output_aliases={0: 0},
        compiler_params=_cparams("parallel", "parallel", "arbitrary"),
        name="ffn_dense",
    )(x, nw.reshape(1, d), sc, sh, g, w_gu_bf16, w_gu_bf16, w_down_bf16)


CONV_W = 4
TAIL = 8


def _causal_conv(pad_ref, x, w_ref, b_ref, first):
    l = x.shape[0]

    @pl.when(first)
    def _():
        pad_ref[0:TAIL, :] = jnp.zeros((TAIL, x.shape[1]), F32)

    pad_ref[TAIL:TAIL + l, :] = x
    y = b_ref[...] + w_ref[CONV_W - 1:CONV_W, :] * x
    for k in range(CONV_W - 1):
        off = TAIL - (CONV_W - 1) + k
        y = y + w_ref[k:k + 1, :] * pad_ref[off:off + l, :]
    pad_ref[0:TAIL, :] = x[l - TAIL:, :]
    return y


LRU_C = 8.0


def _gelu_tanh(x):
    return 0.5 * x * (1.0 + jnp.tanh(math.sqrt(2.0 / math.pi) * (x + 0.044715 * (x * x * x))))


def _lru_kernel(xbr_ref, ybr_ref, cw_ref, cb_ref, gxw_ref, gxb_ref, gaw_ref, gab_ref, lam_ref,
                o_ref, pad_scr, h_scr):
    i = pl.program_id(1)
    l, w = xbr_ref.shape
    nblk, blk = gxw_ref.shape[0], gxw_ref.shape[1]

    @pl.when(i == 0)
    def _():
        h_scr[...] = jnp.zeros_like(h_scr)

    xb = _causal_conv(pad_scr, xbr_ref[...], cw_ref, cb_ref, i == 0)
    xbb = xb.astype(BF16)
    gx = jnp.concatenate([jnp.dot(xbb[:, n * blk:(n + 1) * blk], gxw_ref[n],
                                  preferred_element_type=F32) for n in range(nblk)], axis=-1)
    ga = jnp.concatenate([jnp.dot(xbb[:, n * blk:(n + 1) * blk], gaw_ref[n],
                                  preferred_element_type=F32) for n in range(nblk)], axis=-1)
    gate_x = _sigmoid(gx + gxb_ref[...])
    gate_a = _sigmoid(ga + gab_ref[...])
    log_a = LRU_C * gate_a * (-_softplus(-lam_ref[...]))
    a = jnp.exp(log_a)
    u = jnp.sqrt(1.0 - jnp.exp(2.0 * log_a)) * gate_x * xb

    row = lax.broadcasted_iota(jnp.int32, (l, 1), 0)
    d = 1
    while d < l:
        a_sh = pltpu.roll(a, d, axis=0)
        u_sh = pltpu.roll(u, d, axis=0)
        valid = row >= d
        u = jnp.where(valid, a * u_sh + u, u)
        a = jnp.where(valid, a * a_sh, a)
        d *= 2
    hs = a * h_scr[0:1, :] + u
    h_scr[...] = jnp.broadcast_to(hs[l - 1:l, :], h_scr.shape)
    o_ref[...] = (_gelu_tanh(ybr_ref[...].astype(F32)) * hs).astype(o_ref.dtype)


def _rglru_core(x_br, y_br, conv_w, conv_b, gx_w, gx_b, ga_w, ga_b, lam, chunk=256):
    b, s, w = x_br.shape
    nblk, blk, _ = gx_w.shape
    row = lambda a: a.reshape(1, w).astype(F32)
    full2 = lambda shape: pl.BlockSpec(shape, lambda bi, i: (0, 0))
    full3 = lambda shape: pl.BlockSpec(shape, lambda bi, i: (0, 0, 0))
    return pl.pallas_call(
        _lru_kernel,
        grid=(b, s // chunk),
        in_specs=[
            pl.BlockSpec((None, chunk, w), lambda bi, i: (bi, i, 0)),
            pl.BlockSpec((None, chunk, w), lambda bi, i: (bi, i, 0)),
            full2((CONV_W, w)), full2((1, w)),
            full3((nblk, blk, blk)), full2((1, w)),
            full3((nblk, blk, blk)), full2((1, w)),
            full2((1, w)),
        ],
        out_specs=pl.BlockSpec((None, chunk, w), lambda bi, i: (bi, i, 0)),
        out_shape=jax.ShapeDtypeStruct((b, s, w), BF16),
        scratch_shapes=[pltpu.VMEM((TAIL + chunk, w), F32), pltpu.VMEM((8, w), F32)],
        compiler_params=_cparams("parallel", "arbitrary"),
        name="rglru_core",
    )(x_br, y_br, conv_w, row(conv_b), gx_w.astype(BF16), row(gx_b), ga_w.astype(BF16), row(ga_b),
      row(lam))


def _rglru_layer(x, nw, sc, sh, g, w_in, conv_w, conv_b, gx_w, gx_b, ga_w, ga_b, lam, w_out):
    w = w_in.shape[1] // 2
    y_br, x_br = _norm_mod_matmul(x, nw, sc, sh, w_in.astype(BF16), [w, w], [BF16, F32])
    out = _rglru_core(x_br, y_br, conv_w, conv_b, gx_w, gx_b, ga_w, ga_b, lam)
    return _outproj_residual(x, out, g, w_out.astype(BF16))


NEG_BIG = -1e30


def _split3(a):
    hi = a.astype(BF16)
    r = a - hi.astype(F32)
    mid = r.astype(BF16)
    lo = (r - mid.astype(F32)).astype(BF16)
    return hi, mid, lo


def _tri(l, kind):
    r = lax.broadcasted_iota(jnp.int32, (l, l), 0)
    c = lax.broadcasted_iota(jnp.int32, (l, l), 1)
    m = {"lower": r >= c, "strict_lower": r > c, "upper": r <= c}[kind]
    return m


def _cumsum_rows(a):
    t = jnp.where(_tri(a.shape[0], "lower"), 1.0, 0.0).astype(BF16)
    return sum(jnp.dot(t, p, preferred_element_type=F32) for p in _split3(a))


def _cumsum_lanes(a):
    t = jnp.where(_tri(a.shape[1], "upper"), 1.0, 0.0).astype(BF16)
    return sum(jnp.dot(p, t, preferred_element_type=F32) for p in _split3(a))


def _mlstm_kernel(q_ref, k_ref, v_ref, o_ref, gc_ref, gr_ref, gbr_ref, gbc_ref, nw_ref, y_ref,
                  c_scr, n_scr, m_scr, *, heads, dk, dv):
    i = pl.program_id(1)
    l = q_ref.shape[0]

    @pl.when(i == 0)
    def _():
        c_scr[...] = jnp.zeros_like(c_scr)
        n_scr[...] = jnp.zeros_like(n_scr)
        m_scr[...] = jnp.zeros_like(m_scr)

    gc = gc_ref[...] + gbr_ref[...]
    gr = gr_ref[...] + gbc_ref[...]
    li_col = gc[:, 0:heads]
    li_row = gr[0:heads, :]
    g_col = _cumsum_rows(-_softplus(-gc))[:, heads:2 * heads]
    g_row = _cumsum_lanes(-_softplus(-gr))[heads:2 * heads, :]
    causal = _tri(l, "lower")
    scale = dk ** -0.5

    for h in range(heads):
        a_col = g_col[:, h:h + 1]
        b_row = li_row[h:h + 1, :] - g_row[h:h + 1, :]
        b_col = li_col[:, h:h + 1] - a_col
        m_prev = m_scr[h, 0:1, 0:1]
        logd = jnp.where(causal, a_col + b_row, NEG_BIG)
        log_inter = a_col + m_prev
        m_row = jnp.maximum(jnp.max(logd, axis=-1, keepdims=True), log_inter)
        dmat = jnp.exp(logd - m_row)
        e_inter = jnp.exp(log_inter - m_row)

        qs = (q_ref[:, h * dk:(h + 1) * dk].astype(F32) * scale).astype(BF16)
        k = k_ref[:, h * dk:(h + 1) * dk]
        v = v_ref[:, h * dv:(h + 1) * dv]
        s = lax.dot_general(qs, k, (((1,), (1,)), ((), ())), preferred_element_type=F32) * dmat
        c_mat = c_scr[h]
        n_vec = n_scr[h, 0:1, :]
        num = (jnp.dot(s.astype(BF16), v, preferred_element_type=F32)
               + e_inter * jnp.dot(qs, c_mat.astype(BF16), preferred_element_type=F32))
        den = (jnp.sum(s, axis=-1, keepdims=True)
               + e_inter * jnp.sum(qs.astype(F32) * n_vec, axis=-1, keepdims=True))
        denom = jnp.maximum(jnp.abs(den), jnp.exp(-m_row))
        hh = num / denom
        hh = hh * lax.rsqrt(jnp.mean(hh * hh, axis=-1, keepdims=True) + RMS_EPS)
        gate = _sigmoid(o_ref[:, h * dv:(h + 1) * dv].astype(F32))
        y_ref[:, h * dv:(h + 1) * dv] = (hh * nw_ref[:, h * dv:(h + 1) * dv] * gate).astype(y_ref.dtype)

        g_last = a_col[l - 1:l, :]
        log_w = g_last + b_col
        m_new = jnp.maximum(g_last + m_prev, jnp.max(log_w, axis=0, keepdims=True))
        kw = k.astype(F32) * jnp.exp(log_w - m_new)
        decay = jnp.exp(g_last + m_prev - m_new)
        c_scr[h] = decay * c_mat + lax.dot_general(
            kw.astype(BF16), v, (((0,), (0,)), ((), ())), preferred_element_type=F32)
        n_scr[h, 0:1, :] = decay * n_vec + jnp.sum(kw, axis=0, keepdims=True)
        m_scr[h] = jnp.broadcast_to(m_new, m_scr.shape[1:])


def _mlstm_core(q, k, v, o, gates, gate_b, norm_w, heads, chunk=128):
    b, s, hk = q.shape
    hv = v.shape[-1]
    dk, dv = hk // heads, hv // heads
    g2 = 2 * heads
    gates_t = jnp.swapaxes(gates, 1, 2)
    gb = gate_b.reshape(1, g2).astype(F32)
    tok = lambda n: pl.BlockSpec((None, chunk, n), lambda bi, i: (bi, i, 0))
    full = lambda shape: pl.BlockSpec(shape, lambda bi, i: (0, 0))
    return pl.pallas_call(
        functools.partial(_mlstm_kernel, heads=heads, dk=dk, dv=dv),
        grid=(b, s // chunk),
        in_specs=[tok(hk), tok(hk), tok(hv), tok(hv), tok(g2),
                  pl.BlockSpec((None, g2, chunk), lambda bi, i: (bi, 0, i)),
                  full((1, g2)), full((g2, 1)), full((1, hv))],
        out_specs=tok(hv),
        out_shape=jax.ShapeDtypeStruct((b, s, hv), BF16),
        scratch_shapes=[pltpu.VMEM((heads, dk, dv), F32), pltpu.VMEM((heads, 8, dk), F32),
                        pltpu.VMEM((heads, 8, LANES), F32)],
        compiler_params=_cparams("parallel", "arbitrary"),
        name="mlstm_core",
    )(q, k, v, o, gates, gates_t, gb, gb.reshape(g2, 1), norm_w.reshape(1, hv).astype(F32))


def _mlstm_layer(x, nw, sc, sh, g, w_in, gate_b, norm_w, w_out, heads):
    hv = w_out.shape[0]
    hk = (w_in.shape[1] - 2 * hv - 2 * heads) // 2
    w_pad = _pad_cols(w_in, 2 * hk + 2 * hv + LANES).astype(BF16)
    q, k, v, o, gates = _norm_mod_matmul(x, nw, sc, sh, w_pad, [hk, hk, hv, hv, LANES],
                                         [BF16, BF16, BF16, BF16, F32])
    y = _mlstm_core(q, k, v, o, gates[..., :2 * heads], gate_b, norm_w, heads)
    return _outproj_residual(x, y, g, w_out.astype(BF16))


def _pair_select(lo_half, a, b):
    return jnp.where(lo_half, a, b)


def _ssd_kernel(z_ref, xbc_ref, dtc_ref, dtr_ref, cw_ref, cb_ref, dbr_ref, dbc_ref, alr_ref, alc_ref,
                dsk_ref, nw_ref, y_ref, pad_scr, st_scr, *, groups, hpg, hd, ns):
    i = pl.program_id(1)
    l = z_ref.shape[0]
    inner = groups * hpg * hd
    gw = hpg * hd
    pair = 2 * hd

    @pl.when(i == 0)
    def _():
        st_scr[...] = jnp.zeros_like(st_scr)

    conv = _causal_conv(pad_scr, xbc_ref[...].astype(F32), cw_ref, cb_ref, i == 0)
    conv = conv * _sigmoid(conv)
    xs = conv[:, 0:inner]
    xs_b = xs.astype(BF16)
    bm = conv[:, inner:inner + groups * ns].astype(BF16)
    cm = conv[:, inner + groups * ns:inner + 2 * groups * ns].astype(BF16)

    dt_col = _softplus(dtc_ref[...] + dbr_ref[...])
    dt_row = _softplus(dtr_ref[...] + dbc_ref[...])
    acum_col = _cumsum_rows(dt_col * (-jnp.exp(alr_ref[...])))
    acum_row = _cumsum_lanes(dt_row * (-jnp.exp(alc_ref[...])))
    a_last = acum_col[l - 1:l, :]
    ea_col = jnp.exp(acum_col)
    ws_col = jnp.exp(a_last - acum_col) * dt_col
    ea_last = jnp.exp(a_last)

    causal = _tri(l, "lower")
    lo_half = lax.broadcasted_iota(jnp.int32, (1, pair), 1) < hd

    for g in range(groups):
        bg = bm[:, g * ns:(g + 1) * ns]
        cg = cm[:, g * ns:(g + 1) * ns]
        cb = lax.dot_general(cg, bg, (((1,), (1,)), ((), ())), preferred_element_type=F32)
        st = st_scr[g]
        y_inter = jnp.dot(cg, st.astype(BF16), preferred_element_type=F32)
        xw_parts, y_parts, dec_parts = [], [], []
        for p in range(hpg // 2):
            h0 = g * hpg + 2 * p
            c0 = g * gw + p * pair
            xp = xs_b[:, c0:c0 + pair]
            ys = []
            for h in (h0, h0 + 1):
                dec = jnp.exp(jnp.where(causal, acum_col[:, h:h + 1] - acum_row[h:h + 1, :], NEG_BIG))
                wts = (cb * dec * dt_row[h:h + 1, :]).astype(BF16)
                ys.append(jnp.dot(wts, xp, preferred_element_type=F32))
            sel = lambda a: _pair_select(lo_half, a[:, h0:h0 + 1], a[:, h0 + 1:h0 + 2])
            y_parts.append(_pair_select(lo_half, ys[0], ys[1])
                           + sel(ea_col) * y_inter[:, p * pair:(p + 1) * pair])
            xw_parts.append((xs[:, c0:c0 + pair] * sel(ws_col)).astype(BF16))
            dec_parts.append(sel(ea_last))
        xw = jnp.concatenate(xw_parts, axis=-1)
        st_scr[g] = jnp.concatenate(dec_parts, axis=-1) * st + lax.dot_general(
            bg, xw, (((0,), (0,)), ((), ())), preferred_element_type=F32)
        yg = jnp.concatenate(y_parts, axis=-1)
        sl = slice(g * gw, (g + 1) * gw)
        yg = yg + dsk_ref[:, sl] * xs[:, sl]
        zg = z_ref[:, sl].astype(F32)
        yg = yg * (zg * _sigmoid(zg))
        yg = yg * lax.rsqrt(jnp.mean(yg * yg, axis=-1, keepdims=True) + RMS_EPS)
        y_ref[:, sl] = (yg * nw_ref[:, sl]).astype(y_ref.dtype)


def _ssd_core(z, xbc, dt, conv_w, conv_b, dt_bias, a_log, d_skip, norm_w, groups, hd, ns, chunk=128):
    b, s, inner = z.shape
    heads = dt.shape[-1]
    hpg = heads // groups
    cdim = xbc.shape[-1]
    dt_t = jnp.swapaxes(dt, 1, 2)
    rowv = lambda a: a.reshape(1, -1).astype(F32)
    colv = lambda a: a.reshape(-1, 1).astype(F32)
    tok = lambda n: pl.BlockSpec((None, chunk, n), lambda bi, i: (bi, i, 0))
    full = lambda shape: pl.BlockSpec(shape, lambda bi, i: (0, 0))
    return pl.pallas_call(
        functools.partial(_ssd_kernel, groups=groups, hpg=hpg, hd=hd, ns=ns),
        grid=(b, s // chunk),
        in_specs=[tok(inner), tok(cdim), tok(heads),
                  pl.BlockSpec((None, heads, chunk), lambda bi, i: (bi, 0, i)),
                  full((CONV_W, cdim)), full((1, cdim)),
                  full((1, heads)), full((heads, 1)), full((1, heads)), full((heads, 1)),
                  full((1, inner)), full((1, inner))],
        out_specs=tok(inner),
        out_shape=jax.ShapeDtypeStruct((b, s, inner), BF16),
        scratch_shapes=[pltpu.VMEM((TAIL + chunk, cdim), F32),
                        pltpu.VMEM((groups, ns, hpg * hd), F32)],
        compiler_params=_cparams("parallel", "arbitrary"),
        name="ssd_core",
    )(z, xbc, dt, dt_t, conv_w, rowv(conv_b), rowv(dt_bias), colv(dt_bias), rowv(a_log), colv(a_log),
      rowv(jnp.repeat(d_skip, hd)), rowv(norm_w))


def _ssd_layer(x, nw, sc, sh, g, w_in, conv_w, conv_b, dt_bias, a_log, d_skip, norm_w, w_out,
               groups, hd, ns):
    inner = w_out.shape[0]
    heads = inner // hd
    cdim = inner + 2 * groups * ns
    w_pad = _pad_cols(w_in, inner + cdim + LANES).astype(BF16)
    z, xbc, dt = _norm_mod_matmul(x, nw, sc, sh, w_pad, [inner, cdim, LANES], [BF16, BF16, F32])
    y = _ssd_core(z, xbc, dt[..., :heads], conv_w, conv_b, dt_bias, a_log, d_skip, norm_w,
                  groups, hd, ns)
    return _outproj_residual(x, y, g, w_out.astype(BF16))


def _rwkv_inproj_kernel(x_ref, nw_ref, sc_ref, sh_ref, mu_ref, wr_ref, wk_ref, wv_ref,
                        wdw_ref, wda_ref, wdg_ref, w0_ref, wlb_ref, a0_ref, alb_ref, glb_ref,
                        r_ref, k_ref, v_ref, lw_ref, a_ref, g_ref, last_scr):
    i = pl.program_id(1)
    tm = x_ref.shape[0]

    @pl.when(i == 0)
    def _():
        last_scr[...] = jnp.zeros_like(last_scr)

    h = _norm_mod(x_ref[...], nw_ref[...], sc_ref[...], sh_ref[...])
    row = lax.broadcasted_iota(jnp.int32, (tm, 1), 0)
    h_prev = jnp.where(row == 0, last_scr[0:1, :], pltpu.roll(h, 1, axis=0))
    last_scr[...] = jnp.broadcast_to(h[tm - 1:tm, :], last_scr.shape)
    xx = h_prev - h

    def proj(b, w_ref):
        xb = (h + xx * mu_ref[b:b + 1, :]).astype(BF16)
        return jnp.dot(xb, w_ref[...], preferred_element_type=F32)

    r_ref[...] = proj(0, wr_ref).astype(r_ref.dtype)
    k_ref[...] = proj(1, wk_ref).astype(k_ref.dtype)
    v_ref[...] = proj(2, wv_ref).astype(v_ref.dtype)
    dw = proj(3, wdw_ref)
    da = proj(4, wda_ref)
    dg = proj(5, wdg_ref)
    z = w0_ref[...] + _bdot(jnp.tanh(dw), wlb_ref[...])
    w_log = -_softplus(-z) - 0.5
    lw_ref[...] = -jnp.exp(w_log)
    a_ref[...] = _sigmoid(a0_ref[...] + _bdot(da, alb_ref[...])).astype(a_ref.dtype)
    g_ref[...] = _bdot(_sigmoid(dg), glb_ref[...]).astype(g_ref.dtype)


def _pad_rows(w, n):
    return jnp.pad(w, ((0, n - w.shape[0]), (0, 0)))


def _round_up(n, m):
    return -(-n // m) * m


def _rwkv_inproj(x, nw, sc, sh, w_in, mu, w0, w_lora_b, a0, a_lora_b, g_lora_b, tm=256):
    b, s, d = x.shape
    nl = [w_lora_b.shape[0], a_lora_b.shape[0], g_lora_b.shape[0]]
    nlp = [_round_up(n, LANES) for n in nl]
    offs = [0, d, 2 * d, 3 * d, 3 * d + nl[0], 3 * d + nl[0] + nl[1]]
    wr, wk, wv = (w_in[:, offs[j]:offs[j] + d].astype(BF16) for j in range(3))
    wl = [_pad_cols(w_in[:, offs[3 + j]:offs[3 + j] + nl[j]], nlp[j]).astype(BF16) for j in range(3)]
    lb = [_pad_rows(m, n).astype(BF16) for m, n in zip((w_lora_b, a_lora_b, g_lora_b), nlp)]
    rowv = lambda a: a.reshape(1, d).astype(F32)
    tok = pl.BlockSpec((None, tm, d), lambda bi, i: (bi, i, 0))
    vec = pl.BlockSpec((None, 1, d), lambda bi, i: (bi, 0, 0))
    full = lambda a: pl.BlockSpec(a.shape, lambda bi, i: (0, 0))
    ins = [x, rowv(nw), sc, sh, mu.astype(F32), wr, wk, wv, wl[0], wl[1], wl[2],
           rowv(w0), lb[0], rowv(a0), lb[1], lb[2]]
    specs = [tok, full(ins[1]), vec, vec] + [full(a) for a in ins[4:]]
    dts = [BF16, BF16, BF16, F32, BF16, BF16]
    return pl.pallas_call(
        _rwkv_inproj_kernel,
        grid=(b, s // tm),
        in_specs=specs,
        out_specs=[tok] * 6,
        out_shape=[jax.ShapeDtypeStruct((b, s, d), dt) for dt in dts],
        scratch_shapes=[pltpu.VMEM((8, d), F32)],
        compiler_params=_cparams("parallel", "arbitrary"),
        name="rwkv_inproj",
    )(*ins)


RW_GN_EPS = 64e-5


def _rwkv_kernel(r_ref, k_ref, v_ref, lw_ref, a_ref, g_ref, kk_ref, ka_ref, rk_ref, lnw_ref, lnb_ref,
                 y_ref, s_scr, *, hd):
    i = pl.program_id(1)
    l, d = r_ref.shape
    pair = 2 * hd
    npairs = d // pair

    @pl.when(i == 0)
    def _():
        s_scr[...] = jnp.zeros_like(s_scr)

    lw = lw_ref[...]
    cum = _cumsum_rows(lw)
    lo1 = lax.broadcasted_iota(jnp.int32, (1, pair), 1) < hd
    r2 = lax.broadcasted_iota(jnp.int32, (2 * l, 1), 0) < l
    c2 = lax.broadcasted_iota(jnp.int32, (1, pair), 1) < hd
    stack_mask = r2 == c2
    rr = lax.broadcasted_iota(jnp.int32, (2 * l, 2 * l), 0)
    cc = lax.broadcasted_iota(jnp.int32, (2 * l, 2 * l), 1)
    same = (rr < l) == (cc < l)
    strict = same & (rr > cc)
    incl = same & (rr >= cc)
    kr = lax.broadcasted_iota(jnp.int32, (pair, pair), 0) < hd
    kc = lax.broadcasted_iota(jnp.int32, (pair, pair), 1) < hd
    bd = kr == kc

    def half_sum(t):
        s_lo = jnp.sum(jnp.where(lo1, t, 0.0), axis=-1, keepdims=True)
        s_hi = jnp.sum(jnp.where(lo1, 0.0, t), axis=-1, keepdims=True)
        return jnp.where(lo1, s_lo, s_hi)

    def stack(t, masked):
        t2 = jnp.concatenate([t, t], axis=0)
        return jnp.where(stack_mask, t2, 0.0) if masked else t2

    nt = (((1,), (1,)), ((), ()))
    tn = (((0,), (0,)), ((), ()))
    for p in range(npairs):
        sl = slice(p * pair, (p + 1) * pair)
        r = r_ref[:, sl].astype(F32)
        k = k_ref[:, sl].astype(F32)
        v = v_ref[:, sl].astype(F32)
        a = a_ref[:, sl].astype(F32)
        cm = cum[:, sl]
        gam = jnp.exp(cm)
        gam_prev = jnp.exp(cm - lw[:, sl])
        inv_gam = jnp.exp(-cm)
        kk = k * kk_ref[:, sl]
        kk = kk / jnp.maximum(jnp.sqrt(half_sum(kk * kk)), 1e-12)
        k2 = k * (1.0 + (a - 1.0) * ka_ref[:, sl])

        at = stack(-kk * gam_prev, True).astype(BF16)
        rt = stack(r * gam, True).astype(BF16)
        bt = (kk * a * inv_gam).astype(BF16)
        kt = (k2 * inv_gam).astype(BF16)
        v_st = stack(v, False)
        v_b = v_st.astype(BF16)
        s0 = s_scr[p]

        lhs = jnp.concatenate([at, rt], axis=0)
        rhs = jnp.concatenate([bt, bt, kt, kt], axis=0)
        big = lax.dot_general(lhs, rhs, nt, preferred_element_type=F32)
        a_ab = jnp.where(strict, big[0:2 * l, 0:2 * l], 0.0)
        a_ak = jnp.where(strict, big[0:2 * l, 2 * l:4 * l], 0.0)
        a_rb = jnp.where(incl, big[2 * l:4 * l, 0:2 * l], 0.0)
        a_rk = jnp.where(incl, big[2 * l:4 * l, 2 * l:4 * l], 0.0)
        ls = lax.dot_general(lhs, s0.astype(BF16), nt, preferred_element_type=F32)

        x_sol = ls[0:2 * l] + jnp.dot(a_ak.astype(BF16), v_b, preferred_element_type=F32)
        m = a_ab.astype(BF16)
        span = 1
        while span < l:
            x_sol = x_sol + jnp.dot(m, x_sol.astype(BF16), preferred_element_type=F32)
            span *= 2
            if span < l:
                m = jnp.dot(m, m, preferred_element_type=F32).astype(BF16)
        u_b = x_sol.astype(BF16)
        y_st = ls[2 * l:4 * l] + jnp.dot(
            jnp.concatenate([a_rb, a_rk], axis=1).astype(BF16),
            jnp.concatenate([u_b, v_b], axis=0), preferred_element_type=F32)

        uv = jnp.concatenate([jnp.where(stack_mask, x_sol, 0.0), jnp.where(stack_mask, v_st, 0.0)],
                             axis=0).astype(BF16)
        bk = jnp.concatenate([bt, bt, kt, kt], axis=0)
        upd = lax.dot_general(uv, bk, tn, preferred_element_type=F32)
        s_scr[p] = (s0 + jnp.where(bd, upd, 0.0)) * gam[l - 1:l, :]

        y = jnp.where(lo1, y_st[0:l], y_st[l:2 * l])
        mean = half_sum(y) * (1.0 / hd)
        yc = y - mean
        var = half_sum(yc * yc) * (1.0 / hd)
        y = yc * lax.rsqrt(var + RW_GN_EPS) * lnw_ref[:, sl] + lnb_ref[:, sl]
        y = y + half_sum(r * k2 * rk_ref[:, sl]) * v
        y_ref[:, sl] = (y * g_ref[:, sl].astype(F32)).astype(y_ref.dtype)


def _rwkv_core(r, k, v, lw, a, g, k_k, k_a, r_k, ln_w, ln_b, hd, chunk=64):
    b, s, d = r.shape
    rowv = lambda t: t.reshape(1, d).astype(F32)
    tok = pl.BlockSpec((None, chunk, d), lambda bi, i: (bi, i, 0))
    full = pl.BlockSpec((1, d), lambda bi, i: (0, 0))
    return pl.pallas_call(
        functools.partial(_rwkv_kernel, hd=hd),
        grid=(b, s // chunk),
        in_specs=[tok] * 6 + [full] * 5,
        out_specs=tok,
        out_shape=jax.ShapeDtypeStruct((b, s, d), BF16),
        scratch_shapes=[pltpu.VMEM((d // (2 * hd), 2 * hd, 2 * hd), F32)],
        compiler_params=_cparams("parallel", "arbitrary"),
        name="rwkv_core",
    )(r, k, v, lw, a, g, rowv(k_k), rowv(k_a), rowv(r_k), rowv(ln_w), rowv(ln_b))


def _rwkv_layer(x, nw, sc, sh, g_mod, w_in, mu, w0, w_lora_b, a0, a_lora_b, g_lora_b, k_k, k_a, r_k,
                ln_w, ln_b, w_out, hd):
    r, k, v, lw, a, g = _rwkv_inproj(x, nw, sc, sh, w_in, mu, w0, w_lora_b, a0, a_lora_b, g_lora_b)
    y = _rwkv_core(r, k, v, lw, a, g, k_k, k_a, r_k, ln_w, ln_b, hd)
    return _outproj_residual(x, y, g_mod, w_out.astype(BF16))


MOE_TILE = 1024
MOE_ROWS = 128


def _router_kernel(x_ref, nw_ref, sc_ref, sh_ref, wr_ref, hb_ref, comb_ref, pos_ref, cnt_ref, *, ne):
    t = x_ref.shape[0]
    h = _norm_mod(x_ref[...], nw_ref[...], sc_ref[...], sh_ref[...])
    hb_ref[...] = h.astype(hb_ref.dtype)
    lane = lax.broadcasted_iota(jnp.int32, (t, LANES), 1)
    logits = jnp.where(lane < ne, _dot3(h, wr_ref[...]), NEG_BIG)
    m1 = jnp.max(logits, axis=-1, keepdims=True)
    i1 = jnp.min(jnp.where(logits == m1, lane, LANES), axis=-1, keepdims=True)
    sel1 = lane == i1
    rest = jnp.where(sel1, NEG_BIG, logits)
    m2 = jnp.max(rest, axis=-1, keepdims=True)
    i2 = jnp.min(jnp.where(rest == m2, lane, LANES), axis=-1, keepdims=True)
    sel2 = lane == i2
    e2 = jnp.exp(m2 - m1)
    w1 = 1.0 / (1.0 + e2)
    comb = jnp.where(sel1, w1, 0.0) + jnp.where(sel2, e2 * w1, 0.0)
    sel = jnp.where(sel1 | sel2, 1.0, 0.0)
    below = jnp.where(_tri(t, "strict_lower"), 1.0, 0.0).astype(BF16)
    slot = jnp.dot(below, sel.astype(BF16), preferred_element_type=F32)
    comb_ref[...] = comb.T
    pos_ref[...] = jnp.where(sel > 0.0, slot, -1.0).T
    cnt = jnp.sum(sel, axis=0, keepdims=True).astype(jnp.int32)
    cnt_ref[...] = jnp.broadcast_to(cnt, cnt_ref.shape)


def _moe_router(x, nw, sc, sh, w_router, tile):
    b, s, d = x.shape
    ne = w_router.shape[1]
    nt = s // tile
    vec = pl.BlockSpec((None, 1, d), lambda bi, i: (bi, 0, 0))
    return pl.pallas_call(
        functools.partial(_router_kernel, ne=ne),
        grid=(b, nt),
        in_specs=[pl.BlockSpec((None, tile, d), lambda bi, i: (bi, i, 0)),
                  pl.BlockSpec((1, d), lambda bi, i: (0, 0)), vec, vec,
                  pl.BlockSpec((d, LANES), lambda bi, i: (0, 0))],
        out_specs=[pl.BlockSpec((None, tile, d), lambda bi, i: (bi, i, 0)),
                   pl.BlockSpec((None, LANES, tile), lambda bi, i: (bi, 0, i)),
                   pl.BlockSpec((None, LANES, tile), lambda bi, i: (bi, 0, i)),
                   pl.BlockSpec((None, None, 8, LANES), lambda bi, i: (bi, i, 0, 0))],
        out_shape=[jax.ShapeDtypeStruct((b, s, d), BF16),
                   jax.ShapeDtypeStruct((b, LANES, s), F32),
                   jax.ShapeDtypeStruct((b, LANES, s), F32),
                   jax.ShapeDtypeStruct((b, nt, 8, LANES), jnp.int32)],
        compiler_params=_cparams("parallel", "parallel"),
        name="moe_router",
    )(x, nw.reshape(1, d), sc, sh, _pad_cols(w_router, LANES))


def _expert_kernel(cnt_ref, x_ref, hb_ref, pos_ref, comb_ref, g_ref, wg_ref, wu_ref, wd_ref, o_ref,
                   hg_scr, yacc_scr, oacc_scr, *, rows, ne):
    bi, ti, e, f = (pl.program_id(a) for a in range(4))
    nf = pl.num_programs(3)
    t = x_ref.shape[0]
    nblk_max = t // rows
    cnt = cnt_ref[(bi * pl.num_programs(1) + ti) * ne + e]

    def one_hot(j):
        slot = pos_ref[pl.ds(e, 1), :]
        want = lax.broadcasted_iota(jnp.int32, (rows, 1), 0) + j * rows
        return slot == want.astype(F32)

    @pl.when((e == 0) & (f == 0))
    def _():
        oacc_scr[...] = jnp.zeros_like(oacc_scr)

    for j in range(nblk_max):
        blk = slice(j * rows, (j + 1) * rows)

        @pl.when(j * rows < cnt)
        def _():
            @pl.when(f == 0)
            def _():
                p = jnp.where(one_hot(j), 1.0, 0.0).astype(BF16)
                hg_scr[blk, :] = jnp.dot(p, hb_ref[...], preferred_element_type=F32).astype(BF16)
                yacc_scr[blk, :] = jnp.zeros((rows, yacc_scr.shape[1]), F32)

            hg = hg_scr[blk, :]
            gate = jnp.dot(hg, wg_ref[...], preferred_element_type=F32)
            up = jnp.dot(hg, wu_ref[...], preferred_element_type=F32)
            act = (gate * _sigmoid(gate) * up).astype(BF16)
            yacc_scr[blk, :] += jnp.dot(act, wd_ref[...], preferred_element_type=F32)

            @pl.when(f == nf - 1)
            def _():
                hit = one_hot(j)
                wrow = comb_ref[pl.ds(e, 1), :]
                wgt = jnp.sum(jnp.where(hit, wrow, 0.0), axis=-1, keepdims=True)
                yw = (yacc_scr[blk, :] * wgt).astype(BF16)
                p = jnp.where(hit, 1.0, 0.0).astype(BF16)
                oacc_scr[...] += lax.dot_general(p, yw, (((0,), (0,)), ((), ())),
                                                 preferred_element_type=F32)

    @pl.when((e == ne - 1) & (f == nf - 1))
    def _():
        o_ref[...] = x_ref[...] + g_ref[...] * oacc_scr[...]


def _moe_experts(x, hb, pos_t, comb_t, counts, g, w_gu_bf16, w_down_bf16, tile, rows, tf=512):
    b, s, d = x.shape
    ne, dff, _ = w_down_bf16.shape
    nf = dff // tf
    nt = s // tile
    grid_spec = pltpu.PrefetchScalarGridSpec(
        num_scalar_prefetch=1,
        grid=(b, nt, ne, nf),
        in_specs=[
            pl.BlockSpec((None, tile, d), lambda bi, i, e, f, c: (bi, i, 0)),
            pl.BlockSpec((None, tile, d), lambda bi, i, e, f, c: (bi, i, 0)),
            pl.BlockSpec((None, 8, tile), lambda bi, i, e, f, c: (bi, 0, i)),
            pl.BlockSpec((None, 8, tile), lambda bi, i, e, f, c: (bi, 0, i)),
            pl.BlockSpec((None, 1, d), lambda bi, i, e, f, c: (bi, 0, 0)),
            pl.BlockSpec((None, d, tf), lambda bi, i, e, f, c: (e, 0, f)),
            pl.BlockSpec((None, d, tf), lambda bi, i, e, f, c: (e, 0, f + nf)),
            pl.BlockSpec((None, tf, d), lambda bi, i, e, f, c: (e, f, 0)),
        ],
        out_specs=pl.BlockSpec((None, tile, d), lambda bi, i, e, f, c: (bi, i, 0)),
        scratch_shapes=[pltpu.VMEM((tile, d), BF16), pltpu.VMEM((tile, d), F32),
                        pltpu.VMEM((tile, d), F32)],
    )
    return pl.pallas_call(
        functools.partial(_expert_kernel, rows=rows, ne=ne),
        grid_spec=grid_spec,
        out_shape=jax.ShapeDtypeStruct((b, s, d), F32),
        input_output_aliases={1: 0},
        compiler_params=_cparams("parallel", "parallel", "arbitrary", "arbitrary"),
        name="moe_experts",
    )(counts, x, hb, pos_t, comb_t, g, w_gu_bf16, w_gu_bf16, w_down_bf16)


def _moe_layer(x, nw, sc, sh, g, w_router, w_gu, w_down):
    s = x.shape[1]
    ne = w_router.shape[1]
    tile = min(MOE_TILE, s)
    hb, comb_t, pos_t, counts = _moe_router(x, nw, sc, sh, w_router, tile)
    counts = counts[:, :, 0, :ne].reshape(-1)
    return _moe_experts(x, hb, pos_t, comb_t, counts, g, w_gu.astype(BF16), w_down.astype(BF16),
                        tile, MOE_ROWS)


def _final_norm_kernel(x_ref, w_ref, o_ref):
    x = x_ref[...]
    ms = jnp.mean(x * x, axis=-1, keepdims=True)
    o_ref[...] = x * lax.rsqrt(ms + RMS_EPS) * w_ref[...]


def _final_norm(x, w, tm=1024):
    b, s, d = x.shape
    return pl.pallas_call(
        _final_norm_kernel,
        grid=(b, s // tm),
        in_specs=[pl.BlockSpec((None, tm, d), lambda bi, i: (bi, i, 0)),
                  pl.BlockSpec((1, d), lambda bi, i: (0, 0))],
        out_specs=pl.BlockSpec((None, tm, d), lambda bi, i: (bi, i, 0)),
        out_shape=jax.ShapeDtypeStruct((b, s, d), F32),
        compiler_params=_cparams("parallel", "parallel"),
        name="final_norm",
    )(x, w.reshape(1, d))


ML_HEADS = 4
SSD_GROUPS, SSD_HEADDIM, SSD_STATE = 4, 64, 128
RW_HEADDIM = 64


def kernel(x, c, ada_w, ada_b, norm_w, final_norm_w, ml_w_in, ml_gate_b, ml_norm_w, ml_w_out, ssd_w_in, ssd_conv_w, ssd_conv_b, ssd_dt_bias, ssd_a_log, ssd_d, ssd_norm_w, ssd_w_out, rw_w_in, rw_mu, rw_w0, rw_w_lora_b, rw_a0, rw_a_lora_b, rw_g_lora_b, rw_k_k, rw_k_a, rw_r_k, rw_ln_w, rw_ln_b, rw_w_out, lru_w_in, lru_conv_w, lru_conv_b, lru_gx_w, lru_gx_b, lru_ga_w, lru_ga_b, lru_lambda, lru_w_out, ffn_w_gu, ffn_w_down, moe_router, moe_w_gu, moe_w_down):
    depth = ada_w.shape[0]
    mod = _ada_modulation(c, ada_w, ada_b)
    for layer in range(depth):
        sh_t, sc_t, g_t, sh_c, sc_c, g_c = [m[:, None, :] for m in jnp.split(mod[layer], 6, axis=-1)]
        kind, j = layer % 4, layer // 4
        nw_t, nw_c = norm_w[layer, 0], norm_w[layer, 1]
        if kind == 0:
            x = _mlstm_layer(x, nw_t, sc_t, sh_t, g_t, ml_w_in[j], ml_gate_b[j], ml_norm_w[j],
                             ml_w_out[j], ML_HEADS)
        elif kind == 1:
            x = _ssd_layer(x, nw_t, sc_t, sh_t, g_t, ssd_w_in[j], ssd_conv_w[j], ssd_conv_b[j],
                           ssd_dt_bias[j], ssd_a_log[j], ssd_d[j], ssd_norm_w[j], ssd_w_out[j],
                           SSD_GROUPS, SSD_HEADDIM, SSD_STATE)
        elif kind == 2:
            x = _rwkv_layer(x, nw_t, sc_t, sh_t, g_t, rw_w_in[j], rw_mu[j], rw_w0[j], rw_w_lora_b[j],
                            rw_a0[j], rw_a_lora_b[j], rw_g_lora_b[j], rw_k_k[j], rw_k_a[j], rw_r_k[j],
                            rw_ln_w[j], rw_ln_b[j], rw_w_out[j], RW_HEADDIM)
        else:
            x = _rglru_layer(x, nw_t, sc_t, sh_t, g_t, lru_w_in[j], lru_conv_w[j], lru_conv_b[j],
                             lru_gx_w[j], lru_gx_b[j], lru_ga_w[j], lru_ga_b[j], lru_lambda[j],
                             lru_w_out[j])
        if layer % 2 == 0:
            x = _ffn_dense(x, nw_c, sc_c, sh_c, g_c, ffn_w_gu[layer // 2].astype(BF16),
                           ffn_w_down[layer // 2].astype(BF16))
        else:
            x = _moe_layer(x, nw_c, sc_c, sh_c, g_c, moe_router[layer // 2], moe_w_gu[layer // 2],
                           moe_w_down[layer // 2])
    return _final_norm(x, final_norm_w)
```

```python
import functools
import math

import jax
import jax.numpy as jnp
from jax import lax
from jax.experimental import pallas as pl
from jax.experimental.pallas import tpu as pltpu

F32 = jnp.float32
BF16 = jnp.bfloat16

RMS_EPS = 1e-6
LANES = 128
VMEM_LIMIT_BYTES = 56 * 1024 * 1024


def _cparams(*sem):
    return pltpu.CompilerParams(dimension_semantics=sem, vmem_limit_bytes=VMEM_LIMIT_BYTES)


def _sigmoid(x):
    return 1.0 / (1.0 + jnp.exp(-x))


def _softplus(x):
    return jnp.maximum(x, 0.0) + jnp.log(1.0 + jnp.exp(-jnp.abs(x)))


def _bdot(a, b):
    return jnp.dot(a.astype(BF16), b.astype(BF16), preferred_element_type=F32)


def _split_bf16(a):
    hi = a.astype(BF16)
    lo = (a - hi.astype(F32)).astype(BF16)
    return hi, lo


def _dot3(a, b):
    a_hi, a_lo = _split_bf16(a)
    b_hi, b_lo = _split_bf16(b)
    d = functools.partial(jnp.dot, preferred_element_type=F32)
    return d(a_hi, b_hi) + d(a_lo, b_hi) + d(a_hi, b_lo)


def _norm_mod(x, nw, sc, sh):
    ms = jnp.mean(x * x, axis=-1, keepdims=True)
    y = x * lax.rsqrt(ms + RMS_EPS) * nw
    return y * (1.0 + sc) + sh


def _pad_cols(w, n):
    return jnp.pad(w, ((0, 0), (0, n - w.shape[1])))


def _ada_kernel(c_ref, w_ref, b_ref, o_ref):
    c = c_ref[...]
    cond = c * _sigmoid(c)
    o_ref[...] = _dot3(cond, w_ref[...]) + b_ref[...]


def _ada_modulation(c, ada_w, ada_b):
    depth, d, n = ada_w.shape
    b = c.shape[0]
    rows = 8
    cp = jnp.pad(c, ((0, rows - b), (0, 0)))
    tn = 1024
    out = pl.pallas_call(
        _ada_kernel,
        grid=(depth, n // tn),
        in_specs=[
            pl.BlockSpec((rows, d), lambda l, j: (0, 0)),
            pl.BlockSpec((None, d, tn), lambda l, j: (l, 0, j)),
            pl.BlockSpec((None, 1, tn), lambda l, j: (l, 0, j)),
        ],
        out_specs=pl.BlockSpec((None, rows, tn), lambda l, j: (l, 0, j)),
        out_shape=jax.ShapeDtypeStruct((depth, rows, n), F32),
        compiler_params=_cparams("parallel", "parallel"),
        name="ada_modulation",
    )(cp, ada_w, ada_b.reshape(depth, 1, n))
    return out[:, :b]


def _nmm_kernel(x_ref, nw_ref, sc_ref, sh_ref, w_ref, *o_refs, segs):
    hb = _norm_mod(x_ref[...], nw_ref[...], sc_ref[...], sh_ref[...]).astype(BF16)
    off = 0
    for o_ref, n in zip(o_refs, segs):
        o_ref[...] = jnp.dot(hb, w_ref[:, off:off + n],
                             preferred_element_type=F32).astype(o_ref.dtype)
        off += n


def _norm_mod_matmul(x, nw, sc, sh, w_bf16, segs, dtypes, tm=512):
    b, s, d = x.shape
    ntot = w_bf16.shape[1]
    assert sum(segs) == ntot and s % tm == 0
    vec = pl.BlockSpec((None, 1, d), lambda bi, i: (bi, 0, 0))
    return pl.pallas_call(
        functools.partial(_nmm_kernel, segs=tuple(segs)),
        grid=(b, s // tm),
        in_specs=[
            pl.BlockSpec((None, tm, d), lambda bi, i: (bi, i, 0)),
            pl.BlockSpec((1, d), lambda bi, i: (0, 0)),
            vec, vec,
            pl.BlockSpec((d, ntot), lambda bi, i: (0, 0)),
        ],
        out_specs=[pl.BlockSpec((None, tm, n), lambda bi, i: (bi, i, 0)) for n in segs],
        out_shape=[jax.ShapeDtypeStruct((b, s, n), dt) for n, dt in zip(segs, dtypes)],
        compiler_params=_cparams("parallel", "parallel"),
        name="norm_mod_matmul",
    )(x, nw.reshape(1, d), sc, sh, w_bf16)


def _outproj_kernel(x_ref, y_ref, g_ref, w_ref, o_ref):
    o_ref[...] = x_ref[...] + g_ref[...] * jnp.dot(
        y_ref[...], w_ref[...], preferred_element_type=F32)


def _outproj_residual(x, y, g, w_bf16, tm=512):
    b, s, d = x.shape
    k = y.shape[-1]
    return pl.pallas_call(
        _outproj_kernel,
        grid=(b, s // tm),
        in_specs=[
            pl.BlockSpec((None, tm, d), lambda bi, i: (bi, i, 0)),
            pl.BlockSpec((None, tm, k), lambda bi, i: (bi, i, 0)),
            pl.BlockSpec((None, 1, d), lambda bi, i: (bi, 0, 0)),
            pl.BlockSpec((k, d), lambda bi, i: (0, 0)),
        ],
        out_specs=pl.BlockSpec((None, tm, d), lambda bi, i: (bi, i, 0)),
        out_shape=jax.ShapeDtypeStruct((b, s, d), F32),
        input_output_aliases={0: 0},
        compiler_params=_cparams("parallel", "parallel"),
        name="outproj_residual",
    )(x, y, g, w_bf16)


def _ffn_kernel(x_ref, nw_ref, sc_ref, sh_ref, g_ref, wg_ref, wu_ref, wd_ref, o_ref,
                h_scr, acc_scr):
    f = pl.program_id(2)

    @pl.when(f == 0)
    def _():
        h_scr[...] = _norm_mod(x_ref[...], nw_ref[...], sc_ref[...], sh_ref[...]).astype(BF16)
        acc_scr[...] = jnp.zeros_like(acc_scr)

    hb = h_scr[...]
    gate = jnp.dot(hb, wg_ref[...], preferred_element_type=F32)
    up = jnp.dot(hb, wu_ref[...], preferred_element_type=F32)
    act = (gate * _sigmoid(gate) * up).astype(BF16)
    acc_scr[...] += jnp.dot(act, wd_ref[...], preferred_element_type=F32)

    @pl.when(f == pl.num_programs(2) - 1)
    def _():
        o_ref[...] = x_ref[...] + g_ref[...] * acc_scr[...]


def _ffn_dense(x, nw, sc, sh, g, w_gu_bf16, w_down_bf16, tm=1024, tf=512):
    b, s, d = x.shape
    dff = w_down_bf16.shape[0]
    nf = dff // tf
    vec = pl.BlockSpec((None, 1, d), lambda bi, i, f: (bi, 0, 0))
    return pl.pallas_call(
        _ffn_kernel,
        grid=(b, s // tm, nf),
        in_specs=[
            pl.BlockSpec((None, tm, d), lambda bi, i, f: (bi, i, 0)),
            pl.BlockSpec((1, d), lambda bi, i, f: (0, 0)),
            vec, vec, vec,
            pl.BlockSpec((d, tf), lambda bi, i, f: (0, f)),
            pl.BlockSpec((d, tf), lambda bi, i, f: (0, f + nf)),
            pl.BlockSpec((tf, d), lambda bi, i, f: (f, 0)),
        ],
        out_specs=pl.BlockSpec((None, tm, d), lambda bi, i, f: (bi, i, 0)),
        out_shape=jax.ShapeDtypeStruct((b, s, d), F32),
        scratch_shapes=[pltpu.VMEM((tm, d), BF16), pltpu.VMEM((tm, d), F32)],
        input_output_aliases={0: 0},
        compiler_params=_cparams("parallel", "parallel", "arbitrary"),
        name="ffn_dense",
    )(x, nw.reshape(1, d), sc, sh, g, w_gu_bf16, w_gu_bf16, w_down_bf16)


CONV_W = 4
TAIL = 8


def _causal_conv(pad_ref, x, w_ref, b_ref, first):
    l = x.shape[0]

    @pl.when(first)
    def _():
        pad_ref[0:TAIL, :] = jnp.zeros((TAIL, x.shape[1]), F32)

    pad_ref[TAIL:TAIL + l, :] = x
    y = b_ref[...] + w_ref[CONV_W - 1:CONV_W, :] * x
    for k in range(CONV_W - 1):
        off = TAIL - (CONV_W - 1) + k
        y = y + w_ref[k:k + 1, :] * pad_ref[off:off + l, :]
    pad_ref[0:TAIL, :] = x[l - TAIL:, :]
    return y


LRU_C = 8.0


def _gelu_tanh(x):
    return 0.5 * x * (1.0 + jnp.tanh(math.sqrt(2.0 / math.pi) * (x + 0.044715 * (x * x * x))))


def _lru_kernel(xbr_ref, ybr_ref, cw_ref, cb_ref, gxw_ref, gxb_ref, gaw_ref, gab_ref, lam_ref,
                o_ref, pad_scr, h_scr):
    i = pl.program_id(1)
    l, w = xbr_ref.shape
    nblk, blk = gxw_ref.shape[0], gxw_ref.shape[1]

    @pl.when(i == 0)
    def _():
        h_scr[...] = jnp.zeros_like(h_scr)

    xb = _causal_conv(pad_scr, xbr_ref[...], cw_ref, cb_ref, i == 0)
    xbb = xb.astype(BF16)
    gx = jnp.concatenate([jnp.dot(xbb[:, n * blk:(n + 1) * blk], gxw_ref[n],
                                  preferred_element_type=F32) for n in range(nblk)], axis=-1)
    ga = jnp.concatenate([jnp.dot(xbb[:, n * blk:(n + 1) * blk], gaw_ref[n],
                                  preferred_element_type=F32) for n in range(nblk)], axis=-1)
    gate_x = _sigmoid(gx + gxb_ref[...])
    gate_a = _sigmoid(ga + gab_ref[...])
    log_a = LRU_C * gate_a * (-_softplus(-lam_ref[...]))
    a = jnp.exp(log_a)
    u = jnp.sqrt(1.0 - jnp.exp(2.0 * log_a)) * gate_x * xb

    row = lax.broadcasted_iota(jnp.int32, (l, 1), 0)
    d = 1
    while d < l:
        a_sh = pltpu.roll(a, d, axis=0)
        u_sh = pltpu.roll(u, d, axis=0)
        valid = row >= d
        u = jnp.where(valid, a * u_sh + u, u)
        a = jnp.where(valid, a * a_sh, a)
        d *= 2
    hs = a * h_scr[0:1, :] + u
    h_scr[...] = jnp.broadcast_to(hs[l - 1:l, :], h_scr.shape)
    o_ref[...] = (_gelu_tanh(ybr_ref[...].astype(F32)) * hs).astype(o_ref.dtype)


def _rglru_core(x_br, y_br, conv_w, conv_b, gx_w, gx_b, ga_w, ga_b, lam, chunk=256):
    b, s, w = x_br.shape
    nblk, blk, _ = gx_w.shape
    row = lambda a: a.reshape(1, w).astype(F32)
    full2 = lambda shape: pl.BlockSpec(shape, lambda bi, i: (0, 0))
    full3 = lambda shape: pl.BlockSpec(shape, lambda bi, i: (0, 0, 0))
    return pl.pallas_call(
        _lru_kernel,
        grid=(b, s // chunk),
        in_specs=[
            pl.BlockSpec((None, chunk, w), lambda bi, i: (bi, i, 0)),
            pl.BlockSpec((None, chunk, w), lambda bi, i: (bi, i, 0)),
            full2((CONV_W, w)), full2((1, w)),
            full3((nblk, blk, blk)), full2((1, w)),
            full3((nblk, blk, blk)), full2((1, w)),
            full2((1, w)),
        ],
        out_specs=pl.BlockSpec((None, chunk, w), lambda bi, i: (bi, i, 0)),
        out_shape=jax.ShapeDtypeStruct((b, s, w), BF16),
        scratch_shapes=[pltpu.VMEM((TAIL + chunk, w), F32), pltpu.VMEM((8, w), F32)],
        compiler_params=_cparams("parallel", "arbitrary"),
        name="rglru_core",
    )(x_br, y_br, conv_w, row(conv_b), gx_w.astype(BF16), row(gx_b), ga_w.astype(BF16), row(ga_b),
      row(lam))


def _rglru_layer(x, nw, sc, sh, g, w_in, conv_w, conv_b, gx_w, gx_b, ga_w, ga_b, lam, w_out):
    w = w_in.shape[1] // 2
    y_br, x_br = _norm_mod_matmul(x, nw, sc, sh, w_in.astype(BF16), [w, w], [BF16, F32])
    out = _rglru_core(x_br, y_br, conv_w, conv_b, gx_w, gx_b, ga_w, ga_b, lam)
    return _outproj_residual(x, out, g, w_out.astype(BF16))


NEG_BIG = -1e30


def _split3(a):
    hi = a.astype(BF16)
    r = a - hi.astype(F32)
    mid = r.astype(BF16)
    lo = (r - mid.astype(F32)).astype(BF16)
    return hi, mid, lo


def _tri(l, kind):
    r = lax.broadcasted_iota(jnp.int32, (l, l), 0)
    c = lax.broadcasted_iota(jnp.int32, (l, l), 1)
    m = {"lower": r >= c, "strict_lower": r > c, "upper": r <= c}[kind]
    return m


def _cumsum_rows(a):
    t = jnp.where(_tri(a.shape[0], "lower"), 1.0, 0.0).astype(BF16)
    return sum(jnp.dot(t, p, preferred_element_type=F32) for p in _split3(a))


def _cumsum_lanes(a):
    t = jnp.where(_tri(a.shape[1], "upper"), 1.0, 0.0).astype(BF16)
    return sum(jnp.dot(p, t, preferred_element_type=F32) for p in _split3(a))


def _mlstm_kernel(q_ref, k_ref, v_ref, o_ref, gc_ref, gr_ref, gbr_ref, gbc_ref, nw_ref, y_ref,
                  c_scr, n_scr, m_scr, *, heads, dk, dv):
    i = pl.program_id(1)
    l = q_ref.shape[0]

    @pl.when(i == 0)
    def _():
        c_scr[...] = jnp.zeros_like(c_scr)
        n_scr[...] = jnp.zeros_like(n_scr)
        m_scr[...] = jnp.zeros_like(m_scr)

    gc = gc_ref[...] + gbr_ref[...]
    gr = gr_ref[...] + gbc_ref[...]
    li_col = gc[:, 0:heads]
    li_row = gr[0:heads, :]
    g_col = _cumsum_rows(-_softplus(-gc))[:, heads:2 * heads]
    g_row = _cumsum_lanes(-_softplus(-gr))[heads:2 * heads, :]
    causal = _tri(l, "lower")
    scale = dk ** -0.5

    for h in range(heads):
        a_col = g_col[:, h:h + 1]
        b_row = li_row[h:h + 1, :] - g_row[h:h + 1, :]
        b_col = li_col[:, h:h + 1] - a_col
        m_prev = m_scr[h, 0:1, 0:1]
        logd = jnp.where(causal, a_col + b_row, NEG_BIG)
        log_inter = a_col + m_prev
        m_row = jnp.maximum(jnp.max(logd, axis=-1, keepdims=True), log_inter)
        dmat = jnp.exp(logd - m_row)
        e_inter = jnp.exp(log_inter - m_row)

        qs = (q_ref[:, h * dk:(h + 1) * dk].astype(F32) * scale).astype(BF16)
        k = k_ref[:, h * dk:(h + 1) * dk]
        v = v_ref[:, h * dv:(h + 1) * dv]
        s = lax.dot_general(qs, k, (((1,), (1,)), ((), ())), preferred_element_type=F32) * dmat
        c_mat = c_scr[h]
        n_vec = n_scr[h, 0:1, :]
        num = (jnp.dot(s.astype(BF16), v, preferred_element_type=F32)
               + e_inter * jnp.dot(qs, c_mat.astype(BF16), preferred_element_type=F32))
        den = (jnp.sum(s, axis=-1, keepdims=True)
               + e_inter * jnp.sum(qs.astype(F32) * n_vec, axis=-1, keepdims=True))
        denom = jnp.maximum(jnp.abs(den), jnp.exp(-m_row))
        hh = num / denom
        hh = hh * lax.rsqrt(jnp.mean(hh * hh, axis=-1, keepdims=True) + RMS_EPS)
        gate = _sigmoid(o_ref[:, h * dv:(h + 1) * dv].astype(F32))
        y_ref[:, h * dv:(h + 1) * dv] = (hh * nw_ref[:, h * dv:(h + 1) * dv] * gate).astype(y_ref.dtype)

        g_last = a_col[l - 1:l, :]
        log_w = g_last + b_col
        m_new = jnp.maximum(g_last + m_prev, jnp.max(log_w, axis=0, keepdims=True))
        kw = k.astype(F32) * jnp.exp(log_w - m_new)
        decay = jnp.exp(g_last + m_prev - m_new)
        c_scr[h] = decay * c_mat + lax.dot_general(
            kw.astype(BF16), v, (((0,), (0,)), ((), ())), preferred_element_type=F32)
        n_scr[h, 0:1, :] = decay * n_vec + jnp.sum(kw, axis=0, keepdims=True)
        m_scr[h] = jnp.broadcast_to(m_new, m_scr.shape[1:])


def _mlstm_core(q, k, v, o, gates, gate_b, norm_w, heads, chunk=128):
    b, s, hk = q.shape
    hv = v.shape[-1]
    dk, dv = hk // heads, hv // heads
    g2 = 2 * heads
    gates_t = jnp.swapaxes(gates, 1, 2)
    gb = gate_b.reshape(1, g2).astype(F32)
    tok = lambda n: pl.BlockSpec((None, chunk, n), lambda bi, i: (bi, i, 0))
    full = lambda shape: pl.BlockSpec(shape, lambda bi, i: (0, 0))
    return pl.pallas_call(
        functools.partial(_mlstm_kernel, heads=heads, dk=dk, dv=dv),
        grid=(b, s // chunk),
        in_specs=[tok(hk), tok(hk), tok(hv), tok(hv), tok(g2),
                  pl.BlockSpec((None, g2, chunk), lambda bi, i: (bi, 0, i)),
                  full((1, g2)), full((g2, 1)), full((1, hv))],
        out_specs=tok(hv),
        out_shape=jax.ShapeDtypeStruct((b, s, hv), BF16),
        scratch_shapes=[pltpu.VMEM((heads, dk, dv), F32), pltpu.VMEM((heads, 8, dk), F32),
                        pltpu.VMEM((heads, 8, LANES), F32)],
        compiler_params=_cparams("parallel", "arbitrary"),
        name="mlstm_core",
    )(q, k, v, o, gates, gates_t, gb, gb.reshape(g2, 1), norm_w.reshape(1, hv).astype(F32))


def _mlstm_layer(x, nw, sc, sh, g, w_in, gate_b, norm_w, w_out, heads):
    hv = w_out.shape[0]
    hk = (w_in.shape[1] - 2 * hv - 2 * heads) // 2
    w_pad = _pad_cols(w_in, 2 * hk + 2 * hv + LANES).astype(BF16)
    q, k, v, o, gates = _norm_mod_matmul(x, nw, sc, sh, w_pad, [hk, hk, hv, hv, LANES],
                                         [BF16, BF16, BF16, BF16, F32])
    y = _mlstm_core(q, k, v, o, gates[..., :2 * heads], gate_b, norm_w, heads)
    return _outproj_residual(x, y, g, w_out.astype(BF16))


def _pair_select(lo_half, a, b):
    return jnp.where(lo_half, a, b)


def _ssd_kernel(z_ref, xbc_ref, dtc_ref, dtr_ref, cw_ref, cb_ref, dbr_ref, dbc_ref, alr_ref, alc_ref,
                dsk_ref, nw_ref, y_ref, pad_scr, st_scr, *, groups, hpg, hd, ns):
    i = pl.program_id(1)
    l = z_ref.shape[0]
    inner = groups * hpg * hd
    gw = hpg * hd
    pair = 2 * hd

    @pl.when(i == 0)
    def _():
        st_scr[...] = jnp.zeros_like(st_scr)

    conv = _causal_conv(pad_scr, xbc_ref[...].astype(F32), cw_ref, cb_ref, i == 0)
    conv = conv * _sigmoid(conv)
    xs = conv[:, 0:inner]
    xs_b = xs.astype(BF16)
    bm = conv[:, inner:inner + groups * ns].astype(BF16)
    cm = conv[:, inner + groups * ns:inner + 2 * groups * ns].astype(BF16)

    dt_col = _softplus(dtc_ref[...] + dbr_ref[...])
    dt_row = _softplus(dtr_ref[...] + dbc_ref[...])
    acum_col = _cumsum_rows(dt_col * (-jnp.exp(alr_ref[...])))
    acum_row = _cumsum_lanes(dt_row * (-jnp.exp(alc_ref[...])))
    a_last = acum_col[l - 1:l, :]
    ea_col = jnp.exp(acum_col)
    ws_col = jnp.exp(a_last - acum_col) * dt_col
    ea_last = jnp.exp(a_last)

    causal = _tri(l, "lower")
    lo_half = lax.broadcasted_iota(jnp.int32, (1, pair), 1) < hd

    for g in range(groups):
        bg = bm[:, g * ns:(g + 1) * ns]
        cg = cm[:, g * ns:(g + 1) * ns]
        cb = lax.dot_general(cg, bg, (((1,), (1,)), ((), ())), preferred_element_type=F32)
        st = st_scr[g]
        y_inter = jnp.dot(cg, st.astype(BF16), preferred_element_type=F32)
        xw_parts, y_parts, dec_parts = [], [], []
        for p in range(hpg // 2):
            h0 = g * hpg + 2 * p
            c0 = g * gw + p * pair
            xp = xs_b[:, c0:c0 + pair]
            ys = []
            for h in (h0, h0 + 1):
                dec = jnp.exp(jnp.where(causal, acum_col[:, h:h + 1] - acum_row[h:h + 1, :], NEG_BIG))
                wts = (cb * dec * dt_row[h:h + 1, :]).astype(BF16)
                ys.append(jnp.dot(wts, xp, preferred_element_type=F32))
            sel = lambda a: _pair_select(lo_half, a[:, h0:h0 + 1], a[:, h0 + 1:h0 + 2])
            y_parts.append(_pair_select(lo_half, ys[0], ys[1])
                           + sel(ea_col) * y_inter[:, p * pair:(p + 1) * pair])
            xw_parts.append((xs[:, c0:c0 + pair] * sel(ws_col)).astype(BF16))
            dec_parts.append(sel(ea_last))
        xw = jnp.concatenate(xw_parts, axis=-1)
        st_scr[g] = jnp.concatenate(dec_parts, axis=-1) * st + lax.dot_general(
            bg, xw, (((0,), (0,)), ((), ())), preferred_element_type=F32)
        yg = jnp.concatenate(y_parts, axis=-1)
        sl = slice(g * gw, (g + 1) * gw)
        yg = yg + dsk_ref[:, sl] * xs[:, sl]
        zg = z_ref[:, sl].astype(F32)
        yg = yg * (zg * _sigmoid(zg))
        yg = yg * lax.rsqrt(jnp.mean(yg * yg, axis=-1, keepdims=True) + RMS_EPS)
        y_ref[:, sl] = (yg * nw_ref[:, sl]).astype(y_ref.dtype)


def _ssd_core(z, xbc, dt, conv_w, conv_b, dt_bias, a_log, d_skip, norm_w, groups, hd, ns, chunk=128):
    b, s, inner = z.shape
    heads = dt.shape[-1]
    hpg = heads // groups
    cdim = xbc.shape[-1]
    dt_t = jnp.swapaxes(dt, 1, 2)
    rowv = lambda a: a.reshape(1, -1).astype(F32)
    colv = lambda a: a.reshape(-1, 1).astype(F32)
    tok = lambda n: pl.BlockSpec((None, chunk, n), lambda bi, i: (bi, i, 0))
    full = lambda shape: pl.BlockSpec(shape, lambda bi, i: (0, 0))
    return pl.pallas_call(
        functools.partial(_ssd_kernel, groups=groups, hpg=hpg, hd=hd, ns=ns),
        grid=(b, s // chunk),
        in_specs=[tok(inner), tok(cdim), tok(heads),
                  pl.BlockSpec((None, heads, chunk), lambda bi, i: (bi, 0, i)),
                  full((CONV_W, cdim)), full((1, cdim)),
                  full((1, heads)), full((heads, 1)), full((1, heads)), full((heads, 1)),
                  full((1, inner)), full((1, inner))],
        out_specs=tok(inner),
        out_shape=jax.ShapeDtypeStruct((b, s, inner), BF16),
        scratch_shapes=[pltpu.VMEM((TAIL + chunk, cdim), F32),
                        pltpu.VMEM((groups, ns, hpg * hd), F32)],
        compiler_params=_cparams("parallel", "arbitrary"),
        name="ssd_core",
    )(z, xbc, dt, dt_t, conv_w, rowv(conv_b), rowv(dt_bias), colv(dt_bias), rowv(a_log), colv(a_log),
      rowv(jnp.repeat(d_skip, hd)), rowv(norm_w))


def _ssd_layer(x, nw, sc, sh, g, w_in, conv_w, conv_b, dt_bias, a_log, d_skip, norm_w, w_out,
               groups, hd, ns):
    inner = w_out.shape[0]
    heads = inner // hd
    cdim = inner + 2 * groups * ns
    w_pad = _pad_cols(w_in, inner + cdim + LANES).astype(BF16)
    z, xbc, dt = _norm_mod_matmul(x, nw, sc, sh, w_pad, [inner, cdim, LANES], [BF16, BF16, F32])
    y = _ssd_core(z, xbc, dt[..., :heads], conv_w, conv_b, dt_bias, a_log, d_skip, norm_w,
                  groups, hd, ns)
    return _outproj_residual(x, y, g, w_out.astype(BF16))


def _rwkv_inproj_kernel(x_ref, nw_ref, sc_ref, sh_ref, mu_ref, wr_ref, wk_ref, wv_ref,
                        wdw_ref, wda_ref, wdg_ref, w0_ref, wlb_ref, a0_ref, alb_ref, glb_ref,
                        r_ref, k_ref, v_ref, lw_ref, a_ref, g_ref, last_scr):
    i = pl.program_id(1)
    tm = x_ref.shape[0]

    @pl.when(i == 0)
    def _():
        last_scr[...] = jnp.zeros_like(last_scr)

    h = _norm_mod(x_ref[...], nw_ref[...], sc_ref[...], sh_ref[...])
    row = lax.broadcasted_iota(jnp.int32, (tm, 1), 0)
    h_prev = jnp.where(row == 0, last_scr[0:1, :], pltpu.roll(h, 1, axis=0))
    last_scr[...] = jnp.broadcast_to(h[tm - 1:tm, :], last_scr.shape)
    xx = h_prev - h

    def proj(b, w_ref):
        xb = (h + xx * mu_ref[b:b + 1, :]).astype(BF16)
        return jnp.dot(xb, w_ref[...], preferred_element_type=F32)

    r_ref[...] = proj(0, wr_ref).astype(r_ref.dtype)
    k_ref[...] = proj(1, wk_ref).astype(k_ref.dtype)
    v_ref[...] = proj(2, wv_ref).astype(v_ref.dtype)
    dw = proj(3, wdw_ref)
    da = proj(4, wda_ref)
    dg = proj(5, wdg_ref)
    z = w0_ref[...] + _bdot(jnp.tanh(dw), wlb_ref[...])
    w_log = -_softplus(-z) - 0.5
    lw_ref[...] = -jnp.exp(w_log)
    a_ref[...] = _sigmoid(a0_ref[...] + _bdot(da, alb_ref[...])).astype(a_ref.dtype)
    g_ref[...] = _bdot(_sigmoid(dg), glb_ref[...]).astype(g_ref.dtype)


def _pad_rows(w, n):
    return jnp.pad(w, ((0, n - w.shape[0]), (0, 0)))


def _round_up(n, m):
    return -(-n // m) * m


def _rwkv_inproj(x, nw, sc, sh, w_in, mu, w0, w_lora_b, a0, a_lora_b, g_lora_b, tm=256):
    b, s, d = x.shape
    nl = [w_lora_b.shape[0], a_lora_b.shape[0], g_lora_b.shape[0]]
    nlp = [_round_up(n, LANES) for n in nl]
    offs = [0, d, 2 * d, 3 * d, 3 * d + nl[0], 3 * d + nl[0] + nl[1]]
    wr, wk, wv = (w_in[:, offs[j]:offs[j] + d].astype(BF16) for j in range(3))
    wl = [_pad_cols(w_in[:, offs[3 + j]:offs[3 + j] + nl[j]], nlp[j]).astype(BF16) for j in range(3)]
    lb = [_pad_rows(m, n).astype(BF16) for m, n in zip((w_lora_b, a_lora_b, g_lora_b), nlp)]
    rowv = lambda a: a.reshape(1, d).astype(F32)
    tok = pl.BlockSpec((None, tm, d), lambda bi, i: (bi, i, 0))
    vec = pl.BlockSpec((None, 1, d), lambda bi, i: (bi, 0, 0))
    full = lambda a: pl.BlockSpec(a.shape, lambda bi, i: (0, 0))
    ins = [x, rowv(nw), sc, sh, mu.astype(F32), wr, wk, wv, wl[0], wl[1], wl[2],
           rowv(w0), lb[0], rowv(a0), lb[1], lb[2]]
    specs = [tok, full(ins[1]), vec, vec] + [full(a) for a in ins[4:]]
    dts = [BF16, BF16, BF16, F32, BF16, BF16]
    return pl.pallas_call(
        _rwkv_inproj_kernel,
        grid=(b, s // tm),
        in_specs=specs,
        out_specs=[tok] * 6,
        out_shape=[jax.ShapeDtypeStruct((b, s, d), dt) for dt in dts],
        scratch_shapes=[pltpu.VMEM((8, d), F32)],
        compiler_params=_cparams("parallel", "arbitrary"),
        name="rwkv_inproj",
    )(*ins)


RW_GN_EPS = 64e-5


def _rwkv_kernel(r_ref, k_ref, v_ref, lw_ref, a_ref, g_ref, kk_ref, ka_ref, rk_ref, lnw_ref, lnb_ref,
                 y_ref, s_scr, *, hd):
    i = pl.program_id(1)
    l, d = r_ref.shape
    pair = 2 * hd
    npairs = d // pair

    @pl.when(i == 0)
    def _():
        s_scr[...] = jnp.zeros_like(s_scr)

    lw = lw_ref[...]
    cum = _cumsum_rows(lw)
    lo1 = lax.broadcasted_iota(jnp.int32, (1, pair), 1) < hd
    r2 = lax.broadcasted_iota(jnp.int32, (2 * l, 1), 0) < l
    c2 = lax.broadcasted_iota(jnp.int32, (1, pair), 1) < hd
    stack_mask = r2 == c2
    rr = lax.broadcasted_iota(jnp.int32, (2 * l, 2 * l), 0)
    cc = lax.broadcasted_iota(jnp.int32, (2 * l, 2 * l), 1)
    same = (rr < l) == (cc < l)
    strict = same & (rr > cc)
    incl = same & (rr >= cc)
    kr = lax.broadcasted_iota(jnp.int32, (pair, pair), 0) < hd
    kc = lax.broadcasted_iota(jnp.int32, (pair, pair), 1) < hd
    bd = kr == kc

    def half_sum(t):
        s_lo = jnp.sum(jnp.where(lo1, t, 0.0), axis=-1, keepdims=True)
        s_hi = jnp.sum(jnp.where(lo1, 0.0, t), axis=-1, keepdims=True)
        return jnp.where(lo1, s_lo, s_hi)

    def stack(t, masked):
        t2 = jnp.concatenate([t, t], axis=0)
        return jnp.where(stack_mask, t2, 0.0) if masked else t2

    nt = (((1,), (1,)), ((), ()))
    tn = (((0,), (0,)), ((), ()))
    pairs = range(npairs)
    sls = [slice(p * pair, (p + 1) * pair) for p in pairs]
    st = []
    for p in pairs:
        sl = sls[p]
        r = r_ref[:, sl].astype(F32)
        k = k_ref[:, sl].astype(F32)
        v = v_ref[:, sl].astype(F32)
        a = a_ref[:, sl].astype(F32)
        cm = cum[:, sl]
        gam = jnp.exp(cm)
        gam_prev = jnp.exp(cm - lw[:, sl])
        inv_gam = jnp.exp(-cm)
        kk = k * kk_ref[:, sl]
        kk = kk / jnp.maximum(jnp.sqrt(half_sum(kk * kk)), 1e-12)
        k2 = k * (1.0 + (a - 1.0) * ka_ref[:, sl])
        at = stack(-kk * gam_prev, True).astype(BF16)
        rt = stack(r * gam, True).astype(BF16)
        bt = (kk * a * inv_gam).astype(BF16)
        kt = (k2 * inv_gam).astype(BF16)
        v_st = stack(v, False)
        s0 = s_scr[p]
        lhs = jnp.concatenate([at, rt], axis=0)
        bk = jnp.concatenate([bt, bt, kt, kt], axis=0)
        big = lax.dot_general(lhs, bk, nt, preferred_element_type=F32)
        ls = lax.dot_general(lhs, s0.astype(BF16), nt, preferred_element_type=F32)
        st.append(dict(
            v_st=v_st, v_b=v_st.astype(BF16), bk=bk, s0=s0, gam_last=gam[l - 1:l, :],
            bonus=half_sum(r * k2 * rk_ref[:, sl]) * v,
            m=jnp.where(strict, big[0:2 * l, 0:2 * l], 0.0).astype(BF16),
            a_ak=jnp.where(strict, big[0:2 * l, 2 * l:4 * l], 0.0).astype(BF16),
            a_r=jnp.concatenate([jnp.where(incl, big[2 * l:4 * l, 0:2 * l], 0.0),
                                 jnp.where(incl, big[2 * l:4 * l, 2 * l:4 * l], 0.0)],
                                axis=1).astype(BF16),
            ls_a=ls[0:2 * l], ls_r=ls[2 * l:4 * l]))

    for q in st:
        q["x"] = q["ls_a"] + jnp.dot(q["a_ak"], q["v_b"], preferred_element_type=F32)
    span = 1
    while span < l:
        for q in st:
            q["x"] = q["x"] + jnp.dot(q["m"], q["x"].astype(BF16), preferred_element_type=F32)
        span *= 2
        if span < l:
            for q in st:
                q["m"] = jnp.dot(q["m"], q["m"], preferred_element_type=F32).astype(BF16)

    for p in pairs:
        q, sl = st[p], sls[p]
        x_sol = q["x"]
        y_st = q["ls_r"] + jnp.dot(q["a_r"], jnp.concatenate([x_sol.astype(BF16), q["v_b"]], axis=0),
                                   preferred_element_type=F32)
        uv = jnp.concatenate([jnp.where(stack_mask, x_sol, 0.0),
                              jnp.where(stack_mask, q["v_st"], 0.0)], axis=0).astype(BF16)
        upd = lax.dot_general(uv, q["bk"], tn, preferred_element_type=F32)
        s_scr[p] = (q["s0"] + jnp.where(bd, upd, 0.0)) * q["gam_last"]

        y = jnp.where(lo1, y_st[0:l], y_st[l:2 * l])
        mean = half_sum(y) * (1.0 / hd)
        yc = y - mean
        var = half_sum(yc * yc) * (1.0 / hd)
        y = yc * lax.rsqrt(var + RW_GN_EPS) * lnw_ref[:, sl] + lnb_ref[:, sl]
        y_ref[:, sl] = ((y + q["bonus"]) * g_ref[:, sl].astype(F32)).astype(y_ref.dtype)


def _rwkv_core(r, k, v, lw, a, g, k_k, k_a, r_k, ln_w, ln_b, hd, chunk=64):
    b, s, d = r.shape
    rowv = lambda t: t.reshape(1, d).astype(F32)
    tok = pl.BlockSpec((None, chunk, d), lambda bi, i: (bi, i, 0))
    full = pl.BlockSpec((1, d), lambda bi, i: (0, 0))
    return pl.pallas_call(
        functools.partial(_rwkv_kernel, hd=hd),
        grid=(b, s // chunk),
        in_specs=[tok] * 6 + [full] * 5,
        out_specs=tok,
        out_shape=jax.ShapeDtypeStruct((b, s, d), BF16),
        scratch_shapes=[pltpu.VMEM((d // (2 * hd), 2 * hd, 2 * hd), F32)],
        compiler_params=_cparams("parallel", "arbitrary"),
        name="rwkv_core",
    )(r, k, v, lw, a, g, rowv(k_k), rowv(k_a), rowv(r_k), rowv(ln_w), rowv(ln_b))


def _rwkv_layer(x, nw, sc, sh, g_mod, w_in, mu, w0, w_lora_b, a0, a_lora_b, g_lora_b, k_k, k_a, r_k,
                ln_w, ln_b, w_out, hd):
    r, k, v, lw, a, g = _rwkv_inproj(x, nw, sc, sh, w_in, mu, w0, w_lora_b, a0, a_lora_b, g_lora_b)
    y = _rwkv_core(r, k, v, lw, a, g, k_k, k_a, r_k, ln_w, ln_b, hd)
    return _outproj_residual(x, y, g_mod, w_out.astype(BF16))


MOE_TILE = 1024
MOE_ROWS = 128


def _router_kernel(x_ref, nw_ref, sc_ref, sh_ref, wr_ref, hb_ref, comb_ref, pos_ref, cnt_ref, *, ne):
    t = x_ref.shape[0]
    h = _norm_mod(x_ref[...], nw_ref[...], sc_ref[...], sh_ref[...])
    hb_ref[...] = h.astype(hb_ref.dtype)
    lane = lax.broadcasted_iota(jnp.int32, (t, LANES), 1)
    logits = jnp.where(lane < ne, _dot3(h, wr_ref[...]), NEG_BIG)
    m1 = jnp.max(logits, axis=-1, keepdims=True)
    i1 = jnp.min(jnp.where(logits == m1, lane, LANES), axis=-1, keepdims=True)
    sel1 = lane == i1
    rest = jnp.where(sel1, NEG_BIG, logits)
    m2 = jnp.max(rest, axis=-1, keepdims=True)
    i2 = jnp.min(jnp.where(rest == m2, lane, LANES), axis=-1, keepdims=True)
    sel2 = lane == i2
    e2 = jnp.exp(m2 - m1)
    w1 = 1.0 / (1.0 + e2)
    comb = jnp.where(sel1, w1, 0.0) + jnp.where(sel2, e2 * w1, 0.0)
    sel = jnp.where(sel1 | sel2, 1.0, 0.0)
    below = jnp.where(_tri(t, "strict_lower"), 1.0, 0.0).astype(BF16)
    slot = jnp.dot(below, sel.astype(BF16), preferred_element_type=F32)
    comb_ref[...] = comb.T
    pos_ref[...] = jnp.where(sel > 0.0, slot, -1.0).T
    cnt = jnp.sum(sel, axis=0, keepdims=True).astype(jnp.int32)
    cnt_ref[...] = jnp.broadcast_to(cnt, cnt_ref.shape)


def _moe_router(x, nw, sc, sh, w_router, tile):
    b, s, d = x.shape
    ne = w_router.shape[1]
    nt = s // tile
    vec = pl.BlockSpec((None, 1, d), lambda bi, i: (bi, 0, 0))
    return pl.pallas_call(
        functools.partial(_router_kernel, ne=ne),
        grid=(b, nt),
        in_specs=[pl.BlockSpec((None, tile, d), lambda bi, i: (bi, i, 0)),
                  pl.BlockSpec((1, d), lambda bi, i: (0, 0)), vec, vec,
                  pl.BlockSpec((d, LANES), lambda bi, i: (0, 0))],
        out_specs=[pl.BlockSpec((None, tile, d), lambda bi, i: (bi, i, 0)),
                   pl.BlockSpec((None, LANES, tile), lambda bi, i: (bi, 0, i)),
                   pl.BlockSpec((None, LANES, tile), lambda bi, i: (bi, 0, i)),
                   pl.BlockSpec((None, None, 8, LANES), lambda bi, i: (bi, i, 0, 0))],
        out_shape=[jax.ShapeDtypeStruct((b, s, d), BF16),
                   jax.ShapeDtypeStruct((b, LANES, s), F32),
                   jax.ShapeDtypeStruct((b, LANES, s), F32),
                   jax.ShapeDtypeStruct((b, nt, 8, LANES), jnp.int32)],
        compiler_params=_cparams("parallel", "parallel"),
        name="moe_router",
    )(x, nw.reshape(1, d), sc, sh, _pad_cols(w_router, LANES))


def _expert_kernel(cnt_ref, x_ref, hb_ref, pos_ref, comb_ref, g_ref, wg_ref, wu_ref, wd_ref, o_ref,
                   hg_scr, yacc_scr, *, rows, ne):
    bi, ti, e, f = (pl.program_id(a) for a in range(4))
    nf = pl.num_programs(3)
    t = x_ref.shape[0]
    cnt = cnt_ref[(bi * pl.num_programs(1) + ti) * ne + e]
    nblk = lax.shift_right_logical(cnt + (rows - 1), int(math.log2(rows)))

    @pl.when((e == 0) & (f == 0))
    def _():
        o_ref[...] = x_ref[...]

    for n in range(1, t // rows + 1):
        m = n * rows

        @pl.when(nblk == n)
        def _():
            def one_hot():
                slot = pos_ref[pl.ds(e, 1), :]
                want = lax.broadcasted_iota(jnp.int32, (m, 1), 0).astype(F32)
                return slot == want

            @pl.when(f == 0)
            def _():
                p = jnp.where(one_hot(), 1.0, 0.0).astype(BF16)
                hg_scr[0:m, :] = jnp.dot(p, hb_ref[...], preferred_element_type=F32).astype(BF16)

            hg = hg_scr[0:m, :]
            gate = jnp.dot(hg, wg_ref[...], preferred_element_type=F32)
            up = jnp.dot(hg, wu_ref[...], preferred_element_type=F32)
            act = (gate * _sigmoid(gate) * up).astype(BF16)
            part = jnp.dot(act, wd_ref[...], preferred_element_type=F32)

            @pl.when(f == 0)
            def _():
                yacc_scr[0:m, :] = part

            @pl.when(f > 0)
            def _():
                yacc_scr[0:m, :] += part

            @pl.when(f == nf - 1)
            def _():
                hit = one_hot()
                wrow = comb_ref[pl.ds(e, 1), :]
                wgt = jnp.sum(jnp.where(hit, wrow, 0.0), axis=-1, keepdims=True)
                yw = (yacc_scr[0:m, :] * wgt * g_ref[...]).astype(BF16)
                p = jnp.where(hit, 1.0, 0.0).astype(BF16)
                o_ref[...] += lax.dot_general(p, yw, (((0,), (0,)), ((), ())),
                                              preferred_element_type=F32)


def _moe_experts(x, hb, pos_t, comb_t, counts, g, w_gu_bf16, w_down_bf16, tile, rows, tf=896):
    b, s, d = x.shape
    ne, dff, _ = w_down_bf16.shape
    nf = dff // tf
    nt = s // tile
    grid_spec = pltpu.PrefetchScalarGridSpec(
        num_scalar_prefetch=1,
        grid=(b, nt, ne, nf),
        in_specs=[
            pl.BlockSpec((None, tile, d), lambda bi, i, e, f, c: (bi, i, 0)),
            pl.BlockSpec((None, tile, d), lambda bi, i, e, f, c: (bi, i, 0)),
            pl.BlockSpec((None, 8, tile), lambda bi, i, e, f, c: (bi, 0, i)),
            pl.BlockSpec((None, 8, tile), lambda bi, i, e, f, c: (bi, 0, i)),
            pl.BlockSpec((None, 1, d), lambda bi, i, e, f, c: (bi, 0, 0)),
            pl.BlockSpec((None, d, tf), lambda bi, i, e, f, c: (e, 0, f)),
            pl.BlockSpec((None, d, tf), lambda bi, i, e, f, c: (e, 0, f + nf)),
            pl.BlockSpec((None, tf, d), lambda bi, i, e, f, c: (e, f, 0)),
        ],
        out_specs=pl.BlockSpec((None, tile, d), lambda bi, i, e, f, c: (bi, i, 0)),
        scratch_shapes=[pltpu.VMEM((tile, d), BF16), pltpu.VMEM((tile, d), F32)],
    )
    return pl.pallas_call(
        functools.partial(_expert_kernel, rows=rows, ne=ne),
        grid_spec=grid_spec,
        out_shape=jax.ShapeDtypeStruct((b, s, d), F32),
        input_output_aliases={1: 0},
        compiler_params=_cparams("parallel", "parallel", "arbitrary", "arbitrary"),
        name="moe_experts",
    )(counts, x, hb, pos_t, comb_t, g, w_gu_bf16, w_gu_bf16, w_down_bf16)


def _moe_layer(x, nw, sc, sh, g, w_router, w_gu, w_down):
    s = x.shape[1]
    ne = w_router.shape[1]
    tile = min(MOE_TILE, s)
    hb, comb_t, pos_t, counts = _moe_router(x, nw, sc, sh, w_router, tile)
    counts = counts[:, :, 0, :ne].reshape(-1)
    return _moe_experts(x, hb, pos_t, comb_t, counts, g, w_gu.astype(BF16), w_down.astype(BF16),
                        tile, MOE_ROWS)


def _final_norm_kernel(x_ref, w_ref, o_ref):
    x = x_ref[...]
    ms = jnp.mean(x * x, axis=-1, keepdims=True)
    o_ref[...] = x * lax.rsqrt(ms + RMS_EPS) * w_ref[...]


def _final_norm(x, w, tm=1024):
    b, s, d = x.shape
    return pl.pallas_call(
        _final_norm_kernel,
        grid=(b, s // tm),
        in_specs=[pl.BlockSpec((None, tm, d), lambda bi, i: (bi, i, 0)),
                  pl.BlockSpec((1, d), lambda bi, i: (0, 0))],
        out_specs=pl.BlockSpec((None, tm, d), lambda bi, i: (bi, i, 0)),
        out_shape=jax.ShapeDtypeStruct((b, s, d), F32),
        compiler_params=_cparams("parallel", "parallel"),
        name="final_norm",
    )(x, w.reshape(1, d))


ML_HEADS = 4
SSD_GROUPS, SSD_HEADDIM, SSD_STATE = 4, 64, 128
RW_HEADDIM = 64


def kernel(x, c, ada_w, ada_b, norm_w, final_norm_w, ml_w_in, ml_gate_b, ml_norm_w, ml_w_out, ssd_w_in, ssd_conv_w, ssd_conv_b, ssd_dt_bias, ssd_a_log, ssd_d, ssd_norm_w, ssd_w_out, rw_w_in, rw_mu, rw_w0, rw_w_lora_b, rw_a0, rw_a_lora_b, rw_g_lora_b, rw_k_k, rw_k_a, rw_r_k, rw_ln_w, rw_ln_b, rw_w_out, lru_w_in, lru_conv_w, lru_conv_b, lru_gx_w, lru_gx_b, lru_ga_w, lru_ga_b, lru_lambda, lru_w_out, ffn_w_gu, ffn_w_down, moe_router, moe_w_gu, moe_w_down):
    depth = ada_w.shape[0]
    mod = _ada_modulation(c, ada_w, ada_b)
    for layer in range(depth):
        sh_t, sc_t, g_t, sh_c, sc_c, g_c = [m[:, None, :] for m in jnp.split(mod[layer], 6, axis=-1)]
        kind, j = layer % 4, layer // 4
        nw_t, nw_c = norm_w[layer, 0], norm_w[layer, 1]
        if kind == 0:
            x = _mlstm_layer(x, nw_t, sc_t, sh_t, g_t, ml_w_in[j], ml_gate_b[j], ml_norm_w[j],
                             ml_w_out[j], ML_HEADS)
        elif kind == 1:
            x = _ssd_layer(x, nw_t, sc_t, sh_t, g_t, ssd_w_in[j], ssd_conv_w[j], ssd_conv_b[j],
                           ssd_dt_bias[j], ssd_a_log[j], ssd_d[j], ssd_norm_w[j], ssd_w_out[j],
                           SSD_GROUPS, SSD_HEADDIM, SSD_STATE)
        elif kind == 2:
            x = _rwkv_layer(x, nw_t, sc_t, sh_t, g_t, rw_w_in[j], rw_mu[j], rw_w0[j], rw_w_lora_b[j],
                            rw_a0[j], rw_a_lora_b[j], rw_g_lora_b[j], rw_k_k[j], rw_k_a[j], rw_r_k[j],
                            rw_ln_w[j], rw_ln_b[j], rw_w_out[j], RW_HEADDIM)
        else:
            x = _rglru_layer(x, nw_t, sc_t, sh_t, g_t, lru_w_in[j], lru_conv_w[j], lru_conv_b[j],
                             lru_gx_w[j], lru_gx_b[j], lru_ga_w[j], lru_ga_b[j], lru_lambda[j],
                             lru_w_out[j])
        if layer % 2 == 0:
            x = _ffn_dense(x, nw_c, sc_c, sh_c, g_c, ffn_w_gu[layer // 2].astype(BF16),
                           ffn_w_down[layer // 2].astype(BF16))
        else:
            x = _moe_layer(x, nw_c, sc_c, sh_c, g_c, moe_router[layer // 2], moe_w_gu[layer // 2],
                           moe_w_down[layer // 2])
    return _final_norm(x, final_norm_w)
```

```python
import functools
import math

import jax
import jax.numpy as jnp
from jax import lax
from jax.experimental import pallas as pl
from jax.experimental.pallas import tpu as pltpu

F32 = jnp.float32
BF16 = jnp.bfloat16

RMS_EPS = 1e-6
LANES = 128
VMEM_LIMIT_BYTES = 56 * 1024 * 1024


def _cparams(*sem):
    return pltpu.CompilerParams(dimension_semantics=sem, vmem_limit_bytes=VMEM_LIMIT_BYTES)


def _sigmoid(x):
    return 1.0 / (1.0 + jnp.exp(-x))


def _softplus(x):
    return jnp.maximum(x, 0.0) + jnp.log(1.0 + jnp.exp(-jnp.abs(x)))


def _bdot(a, b):
    return jnp.dot(a.astype(BF16), b.astype(BF16), preferred_element_type=F32)


def _split_bf16(a):
    hi = a.astype(BF16)
    lo = (a - hi.astype(F32)).astype(BF16)
    return hi, lo


def _dot3(a, b):
    a_hi, a_lo = _split_bf16(a)
    b_hi, b_lo = _split_bf16(b)
    d = functools.partial(jnp.dot, preferred_element_type=F32)
    return d(a_hi, b_hi) + d(a_lo, b_hi) + d(a_hi, b_lo)


def _norm_mod(x, nw, sc, sh):
    ms = jnp.mean(x * x, axis=-1, keepdims=True)
    y = x * lax.rsqrt(ms + RMS_EPS) * nw
    return y * (1.0 + sc) + sh


def _pad_cols(w, n):
    return jnp.pad(w, ((0, 0), (0, n - w.shape[1])))


def _ada_kernel(c_ref, w_ref, b_ref, o_ref):
    c = c_ref[...]
    cond = c * _sigmoid(c)
    o_ref[...] = _dot3(cond, w_ref[...]) + b_ref[...]


def _ada_modulation(c, ada_w, ada_b):
    depth, d, n = ada_w.shape
    b = c.shape[0]
    rows = 8
    cp = jnp.pad(c, ((0, rows - b), (0, 0)))
    tn = 1024
    out = pl.pallas_call(
        _ada_kernel,
        grid=(depth, n // tn),
        in_specs=[
            pl.BlockSpec((rows, d), lambda l, j: (0, 0)),
            pl.BlockSpec((None, d, tn), lambda l, j: (l, 0, j)),
            pl.BlockSpec((None, 1, tn), lambda l, j: (l, 0, j)),
        ],
        out_specs=pl.BlockSpec((None, rows, tn), lambda l, j: (l, 0, j)),
        out_shape=jax.ShapeDtypeStruct((depth, rows, n), F32),
        compiler_params=_cparams("parallel", "parallel"),
        name="ada_modulation",
    )(cp, ada_w, ada_b.reshape(depth, 1, n))
    return out[:, :b]


def _nmm_kernel(x_ref, nw_ref, sc_ref, sh_ref, w_ref, *o_refs, segs):
    hb = _norm_mod(x_ref[...], nw_ref[...], sc_ref[...], sh_ref[...]).astype(BF16)
    off = 0
    for o_ref, n in zip(o_refs, segs):
        o_ref[...] = jnp.dot(hb, w_ref[:, off:off + n],
                             preferred_element_type=F32).astype(o_ref.dtype)
        off += n


def _norm_mod_matmul(x, nw, sc, sh, w_bf16, segs, dtypes, tm=512):
    b, s, d = x.shape
    ntot = w_bf16.shape[1]
    assert sum(segs) == ntot and s % tm == 0
    vec = pl.BlockSpec((None, 1, d), lambda bi, i: (bi, 0, 0))
    return pl.pallas_call(
        functools.partial(_nmm_kernel, segs=tuple(segs)),
        grid=(b, s // tm),
        in_specs=[
            pl.BlockSpec((None, tm, d), lambda bi, i: (bi, i, 0)),
            pl.BlockSpec((1, d), lambda bi, i: (0, 0)),
            vec, vec,
            pl.BlockSpec((d, ntot), lambda bi, i: (0, 0)),
        ],
        out_specs=[pl.BlockSpec((None, tm, n), lambda bi, i: (bi, i, 0)) for n in segs],
        out_shape=[jax.ShapeDtypeStruct((b, s, n), dt) for n, dt in zip(segs, dtypes)],
        compiler_params=_cparams("parallel", "parallel"),
        name="norm_mod_matmul",
    )(x, nw.reshape(1, d), sc, sh, w_bf16)


def _outproj_kernel(x_ref, y_ref, g_ref, w_ref, o_ref):
    o_ref[...] = x_ref[...] + g_ref[...] * jnp.dot(
        y_ref[...], w_ref[...], preferred_element_type=F32)


def _outproj_residual(x, y, g, w_bf16, tm=512):
    b, s, d = x.shape
    k = y.shape[-1]
    return pl.pallas_call(
        _outproj_kernel,
        grid=(b, s // tm),
        in_specs=[
            pl.BlockSpec((None, tm, d), lambda bi, i: (bi, i, 0)),
            pl.BlockSpec((None, tm, k), lambda bi, i: (bi, i, 0)),
            pl.BlockSpec((None, 1, d), lambda bi, i: (bi, 0, 0)),
            pl.BlockSpec((k, d), lambda bi, i: (0, 0)),
        ],
        out_specs=pl.BlockSpec((None, tm, d), lambda bi, i: (bi, i, 0)),
        out_shape=jax.ShapeDtypeStruct((b, s, d), F32),
        input_output_aliases={0: 0},
        compiler_params=_cparams("parallel", "parallel"),
        name="outproj_residual",
    )(x, y, g, w_bf16)


def _ffn_kernel(x_ref, nw_ref, sc_ref, sh_ref, g_ref, wg_ref, wu_ref, wd_ref, o_ref,
                h_scr, acc_scr):
    f = pl.program_id(2)

    @pl.when(f == 0)
    def _():
        h_scr[...] = _norm_mod(x_ref[...], nw_ref[...], sc_ref[...], sh_ref[...]).astype(BF16)
        acc_scr[...] = jnp.zeros_like(acc_scr)

    hb = h_scr[...]
    gate = jnp.dot(hb, wg_ref[...], preferred_element_type=F32)
    up = jnp.dot(hb, wu_ref[...], preferred_element_type=F32)
    act = (gate * _sigmoid(gate) * up).astype(BF16)
    acc_scr[...] += jnp.dot(act, wd_ref[...], preferred_element_type=F32)

    @pl.when(f == pl.num_programs(2) - 1)
    def _():
        o_ref[...] = x_ref[...] + g_ref[...] * acc_scr[...]


def _ffn_dense(x, nw, sc, sh, g, w_gu_bf16, w_down_bf16, tm=1024, tf=512):
    b, s, d = x.shape
    dff = w_down_bf16.shape[0]
    nf = dff // tf
    vec = pl.BlockSpec((None, 1, d), lambda bi, i, f: (bi, 0, 0))
    return pl.pallas_call(
        _ffn_kernel,
        grid=(b, s // tm, nf),
        in_specs=[
            pl.BlockSpec((None, tm, d), lambda bi, i, f: (bi, i, 0)),
            pl.BlockSpec((1, d), lambda bi, i, f: (0, 0)),
            vec, vec, vec,
            pl.BlockSpec((d, tf), lambda bi, i, f: (0, f)),
            pl.BlockSpec((d, tf), lambda bi, i, f: (0, f + nf)),
            pl.BlockSpec((tf, d), lambda bi, i, f: (f, 0)),
        ],
        out_specs=pl.BlockSpec((None, tm, d), lambda bi, i, f: (bi, i, 0)),
        out_shape=jax.ShapeDtypeStruct((b, s, d), F32),
        scratch_shapes=[pltpu.VMEM((tm, d), BF16), pltpu.VMEM((tm, d), F32)],
        input_output_aliases={0: 0},
        compiler_params=_cparams("parallel", "parallel", "arbitrary"),
        name="ffn_dense",
    )(x, nw.reshape(1, d), sc, sh, g, w_gu_bf16, w_gu_bf16, w_down_bf16)


CONV_W = 4
TAIL = 8


def _causal_conv(pad_ref, x, w_ref, b_ref, first):
    l = x.shape[0]

    @pl.when(first)
    def _():
        pad_ref[0:TAIL, :] = jnp.zeros((TAIL, x.shape[1]), F32)

    pad_ref[TAIL:TAIL + l, :] = x
    y = b_ref[...] + w_ref[CONV_W - 1:CONV_W, :] * x
    for k in range(CONV_W - 1):
        off = TAIL - (CONV_W - 1) + k
        y = y + w_ref[k:k + 1, :] * pad_ref[off:off + l, :]
    pad_ref[0:TAIL, :] = x[l - TAIL:, :]
    return y


LRU_C = 8.0


def _gelu_tanh(x):
    return 0.5 * x * (1.0 + jnp.tanh(math.sqrt(2.0 / math.pi) * (x + 0.044715 * (x * x * x))))


def _lru_kernel(xbr_ref, ybr_ref, cw_ref, cb_ref, gxw_ref, gxb_ref, gaw_ref, gab_ref, lam_ref,
                o_ref, pad_scr, h_scr):
    i = pl.program_id(1)
    l, w = xbr_ref.shape
    nblk, blk = gxw_ref.shape[0], gxw_ref.shape[1]

    @pl.when(i == 0)
    def _():
        h_scr[...] = jnp.zeros_like(h_scr)

    xb = _causal_conv(pad_scr, xbr_ref[...], cw_ref, cb_ref, i == 0)
    xbb = xb.astype(BF16)
    gx = jnp.concatenate([jnp.dot(xbb[:, n * blk:(n + 1) * blk], gxw_ref[n],
                                  preferred_element_type=F32) for n in range(nblk)], axis=-1)
    ga = jnp.concatenate([jnp.dot(xbb[:, n * blk:(n + 1) * blk], gaw_ref[n],
                                  preferred_element_type=F32) for n in range(nblk)], axis=-1)
    gate_x = _sigmoid(gx + gxb_ref[...])
    gate_a = _sigmoid(ga + gab_ref[...])
    log_a = LRU_C * gate_a * (-_softplus(-lam_ref[...]))
    a = jnp.exp(log_a)
    u = jnp.sqrt(1.0 - jnp.exp(2.0 * log_a)) * gate_x * xb

    row = lax.broadcasted_iota(jnp.int32, (l, 1), 0)
    d = 1
    while d < l:
        a_sh = pltpu.roll(a, d, axis=0)
        u_sh = pltpu.roll(u, d, axis=0)
        valid = row >= d
        u = jnp.where(valid, a * u_sh + u, u)
        a = jnp.where(valid, a * a_sh, a)
        d *= 2
    hs = a * h_scr[0:1, :] + u
    h_scr[...] = jnp.broadcast_to(hs[l - 1:l, :], h_scr.shape)
    o_ref[...] = (_gelu_tanh(ybr_ref[...].astype(F32)) * hs).astype(o_ref.dtype)


def _rglru_core(x_br, y_br, conv_w, conv_b, gx_w, gx_b, ga_w, ga_b, lam, chunk=256):
    b, s, w = x_br.shape
    nblk, blk, _ = gx_w.shape
    row = lambda a: a.reshape(1, w).astype(F32)
    full2 = lambda shape: pl.BlockSpec(shape, lambda bi, i: (0, 0))
    full3 = lambda shape: pl.BlockSpec(shape, lambda bi, i: (0, 0, 0))
    return pl.pallas_call(
        _lru_kernel,
        grid=(b, s // chunk),
        in_specs=[
            pl.BlockSpec((None, chunk, w), lambda bi, i: (bi, i, 0)),
            pl.BlockSpec((None, chunk, w), lambda bi, i: (bi, i, 0)),
            full2((CONV_W, w)), full2((1, w)),
            full3((nblk, blk, blk)), full2((1, w)),
            full3((nblk, blk, blk)), full2((1, w)),
            full2((1, w)),
        ],
        out_specs=pl.BlockSpec((None, chunk, w), lambda bi, i: (bi, i, 0)),
        out_shape=jax.ShapeDtypeStruct((b, s, w), BF16),
        scratch_shapes=[pltpu.VMEM((TAIL + chunk, w), F32), pltpu.VMEM((8, w), F32)],
        compiler_params=_cparams("parallel", "arbitrary"),
        name="rglru_core",
    )(x_br, y_br, conv_w, row(conv_b), gx_w.astype(BF16), row(gx_b), ga_w.astype(BF16), row(ga_b),
      row(lam))


def _rglru_layer(x, nw, sc, sh, g, w_in, conv_w, conv_b, gx_w, gx_b, ga_w, ga_b, lam, w_out):
    w = w_in.shape[1] // 2
    y_br, x_br = _norm_mod_matmul(x, nw, sc, sh, w_in.astype(BF16), [w, w], [BF16, F32])
    out = _rglru_core(x_br, y_br, conv_w, conv_b, gx_w, gx_b, ga_w, ga_b, lam)
    return _outproj_residual(x, out, g, w_out.astype(BF16))


NEG_BIG = -1e30


def _split3(a):
    hi = a.astype(BF16)
    r = a - hi.astype(F32)
    mid = r.astype(BF16)
    lo = (r - mid.astype(F32)).astype(BF16)
    return hi, mid, lo


def _tri(l, kind):
    r = lax.broadcasted_iota(jnp.int32, (l, l), 0)
    c = lax.broadcasted_iota(jnp.int32, (l, l), 1)
    m = {"lower": r >= c, "strict_lower": r > c, "upper": r <= c}[kind]
    return m


def _cumsum_rows(a):
    t = jnp.where(_tri(a.shape[0], "lower"), 1.0, 0.0).astype(BF16)
    return sum(jnp.dot(t, p, preferred_element_type=F32) for p in _split3(a))


def _cumsum_lanes(a):
    t = jnp.where(_tri(a.shape[1], "upper"), 1.0, 0.0).astype(BF16)
    return sum(jnp.dot(p, t, preferred_element_type=F32) for p in _split3(a))


def _mlstm_kernel(q_ref, k_ref, v_ref, o_ref, gc_ref, gr_ref, gbr_ref, gbc_ref, nw_ref, y_ref,
                  c_scr, n_scr, m_scr, *, heads, dk, dv):
    i = pl.program_id(1)
    l = q_ref.shape[0]

    @pl.when(i == 0)
    def _():
        c_scr[...] = jnp.zeros_like(c_scr)
        n_scr[...] = jnp.zeros_like(n_scr)
        m_scr[...] = jnp.zeros_like(m_scr)

    gc = gc_ref[...] + gbr_ref[...]
    gr = gr_ref[...] + gbc_ref[...]
    li_col = gc[:, 0:heads]
    li_row = gr[0:heads, :]
    g_col = _cumsum_rows(-_softplus(-gc))[:, heads:2 * heads]
    g_row = _cumsum_lanes(-_softplus(-gr))[heads:2 * heads, :]
    causal = _tri(l, "lower")
    scale = dk ** -0.5

    for h in range(heads):
        a_col = g_col[:, h:h + 1]
        b_row = li_row[h:h + 1, :] - g_row[h:h + 1, :]
        b_col = li_col[:, h:h + 1] - a_col
        m_prev = m_scr[h, 0:1, 0:1]
        logd = jnp.where(causal, a_col + b_row, NEG_BIG)
        log_inter = a_col + m_prev
        m_row = jnp.maximum(jnp.max(logd, axis=-1, keepdims=True), log_inter)
        dmat = jnp.exp(logd - m_row)
        e_inter = jnp.exp(log_inter - m_row)

        qs = (q_ref[:, h * dk:(h + 1) * dk].astype(F32) * scale).astype(BF16)
        k = k_ref[:, h * dk:(h + 1) * dk]
        v = v_ref[:, h * dv:(h + 1) * dv]
        s = lax.dot_general(qs, k, (((1,), (1,)), ((), ())), preferred_element_type=F32) * dmat
        c_mat = c_scr[h]
        n_vec = n_scr[h, 0:1, :]
        num = (jnp.dot(s.astype(BF16), v, preferred_element_type=F32)
               + e_inter * jnp.dot(qs, c_mat.astype(BF16), preferred_element_type=F32))
        den = (jnp.sum(s, axis=-1, keepdims=True)
               + e_inter * jnp.sum(qs.astype(F32) * n_vec, axis=-1, keepdims=True))
        denom = jnp.maximum(jnp.abs(den), jnp.exp(-m_row))
        hh = num / denom
        hh = hh * lax.rsqrt(jnp.mean(hh * hh, axis=-1, keepdims=True) + RMS_EPS)
        gate = _sigmoid(o_ref[:, h * dv:(h + 1) * dv].astype(F32))
        y_ref[:, h * dv:(h + 1) * dv] = (hh * nw_ref[:, h * dv:(h + 1) * dv] * gate).astype(y_ref.dtype)

        g_last = a_col[l - 1:l, :]
        log_w = g_last + b_col
        m_new = jnp.maximum(g_last + m_prev, jnp.max(log_w, axis=0, keepdims=True))
        kw = k.astype(F32) * jnp.exp(log_w - m_new)
        decay = jnp.exp(g_last + m_prev - m_new)
        c_scr[h] = decay * c_mat + lax.dot_general(
            kw.astype(BF16), v, (((0,), (0,)), ((), ())), preferred_element_type=F32)
        n_scr[h, 0:1, :] = decay * n_vec + jnp.sum(kw, axis=0, keepdims=True)
        m_scr[h] = jnp.broadcast_to(m_new, m_scr.shape[1:])


def _mlstm_core(q, k, v, o, gates, gate_b, norm_w, heads, chunk=128):
    b, s, hk = q.shape
    hv = v.shape[-1]
    dk, dv = hk // heads, hv // heads
    g2 = 2 * heads
    gates_t = jnp.swapaxes(gates, 1, 2)
    gb = gate_b.reshape(1, g2).astype(F32)
    tok = lambda n: pl.BlockSpec((None, chunk, n), lambda bi, i: (bi, i, 0))
    full = lambda shape: pl.BlockSpec(shape, lambda bi, i: (0, 0))
    return pl.pallas_call(
        functools.partial(_mlstm_kernel, heads=heads, dk=dk, dv=dv),
        grid=(b, s // chunk),
        in_specs=[tok(hk), tok(hk), tok(hv), tok(hv), tok(g2),
                  pl.BlockSpec((None, g2, chunk), lambda bi, i: (bi, 0, i)),
                  full((1, g2)), full((g2, 1)), full((1, hv))],
        out_specs=tok(hv),
        out_shape=jax.ShapeDtypeStruct((b, s, hv), BF16),
        scratch_shapes=[pltpu.VMEM((heads, dk, dv), F32), pltpu.VMEM((heads, 8, dk), F32),
                        pltpu.VMEM((heads, 8, LANES), F32)],
        compiler_params=_cparams("parallel", "arbitrary"),
        name="mlstm_core",
    )(q, k, v, o, gates, gates_t, gb, gb.reshape(g2, 1), norm_w.reshape(1, hv).astype(F32))


def _mlstm_layer(x, nw, sc, sh, g, w_in, gate_b, norm_w, w_out, heads):
    hv = w_out.shape[0]
    hk = (w_in.shape[1] - 2 * hv - 2 * heads) // 2
    w_pad = _pad_cols(w_in, 2 * hk + 2 * hv + LANES).astype(BF16)
    q, k, v, o, gates = _norm_mod_matmul(x, nw, sc, sh, w_pad, [hk, hk, hv, hv, LANES],
                                         [BF16, BF16, BF16, BF16, F32])
    y = _mlstm_core(q, k, v, o, gates[..., :2 * heads], gate_b, norm_w, heads)
    return _outproj_residual(x, y, g, w_out.astype(BF16))


def _pair_select(lo_half, a, b):
    return jnp.where(lo_half, a, b)


def _ssd_kernel(z_ref, xbc_ref, dtc_ref, dtr_ref, cw_ref, cb_ref, dbr_ref, dbc_ref, alr_ref, alc_ref,
                dsk_ref, nw_ref, y_ref, pad_scr, st_scr, *, groups, hpg, hd, ns):
    i = pl.program_id(1)
    l = z_ref.shape[0]
    inner = groups * hpg * hd
    gw = hpg * hd
    pair = 2 * hd

    @pl.when(i == 0)
    def _():
        st_scr[...] = jnp.zeros_like(st_scr)

    conv = _causal_conv(pad_scr, xbc_ref[...].astype(F32), cw_ref, cb_ref, i == 0)
    conv = conv * _sigmoid(conv)
    xs = conv[:, 0:inner]
    xs_b = xs.astype(BF16)
    bm = conv[:, inner:inner + groups * ns].astype(BF16)
    cm = conv[:, inner + groups * ns:inner + 2 * groups * ns].astype(BF16)

    dt_col = _softplus(dtc_ref[...] + dbr_ref[...])
    dt_row = _softplus(dtr_ref[...] + dbc_ref[...])
    acum_col = _cumsum_rows(dt_col * (-jnp.exp(alr_ref[...])))
    acum_row = _cumsum_lanes(dt_row * (-jnp.exp(alc_ref[...])))
    a_last = acum_col[l - 1:l, :]
    ea_col = jnp.exp(acum_col)
    ws_col = jnp.exp(a_last - acum_col) * dt_col
    ea_last = jnp.exp(a_last)

    causal = _tri(l, "lower")
    lo_half = lax.broadcasted_iota(jnp.int32, (1, pair), 1) < hd

    for g in range(groups):
        bg = bm[:, g * ns:(g + 1) * ns]
        cg = cm[:, g * ns:(g + 1) * ns]
        cb = lax.dot_general(cg, bg, (((1,), (1,)), ((), ())), preferred_element_type=F32)
        st = st_scr[g]
        y_inter = jnp.dot(cg, st.astype(BF16), preferred_element_type=F32)
        xw_parts, y_parts, dec_parts = [], [], []
        for p in range(hpg // 2):
            h0 = g * hpg + 2 * p
            c0 = g * gw + p * pair
            xp = xs_b[:, c0:c0 + pair]
            ys = []
            for h in (h0, h0 + 1):
                dec = jnp.exp(jnp.where(causal, acum_col[:, h:h + 1] - acum_row[h:h + 1, :], NEG_BIG))
                wts = (cb * dec * dt_row[h:h + 1, :]).astype(BF16)
                ys.append(jnp.dot(wts, xp, preferred_element_type=F32))
            sel = lambda a: _pair_select(lo_half, a[:, h0:h0 + 1], a[:, h0 + 1:h0 + 2])
            y_parts.append(_pair_select(lo_half, ys[0], ys[1])
                           + sel(ea_col) * y_inter[:, p * pair:(p + 1) * pair])
            xw_parts.append((xs[:, c0:c0 + pair] * sel(ws_col)).astype(BF16))
            dec_parts.append(sel(ea_last))
        xw = jnp.concatenate(xw_parts, axis=-1)
        st_scr[g] = jnp.concatenate(dec_parts, axis=-1) * st + lax.dot_general(
            bg, xw, (((0,), (0,)), ((), ())), preferred_element_type=F32)
        yg = jnp.concatenate(y_parts, axis=-1)
        sl = slice(g * gw, (g + 1) * gw)
        yg = yg + dsk_ref[:, sl] * xs[:, sl]
        zg = z_ref[:, sl].astype(F32)
        yg = yg * (zg * _sigmoid(zg))
        yg = yg * lax.rsqrt(jnp.mean(yg * yg, axis=-1, keepdims=True) + RMS_EPS)
        y_ref[:, sl] = (yg * nw_ref[:, sl]).astype(y_ref.dtype)


def _ssd_core(z, xbc, dt, conv_w, conv_b, dt_bias, a_log, d_skip, norm_w, groups, hd, ns, chunk=128):
    b, s, inner = z.shape
    heads = dt.shape[-1]
    hpg = heads // groups
    cdim = xbc.shape[-1]
    dt_t = jnp.swapaxes(dt, 1, 2)
    rowv = lambda a: a.reshape(1, -1).astype(F32)
    colv = lambda a: a.reshape(-1, 1).astype(F32)
    tok = lambda n: pl.BlockSpec((None, chunk, n), lambda bi, i: (bi, i, 0))
    full = lambda shape: pl.BlockSpec(shape, lambda bi, i: (0, 0))
    return pl.pallas_call(
        functools.partial(_ssd_kernel, groups=groups, hpg=hpg, hd=hd, ns=ns),
        grid=(b, s // chunk),
        in_specs=[tok(inner), tok(cdim), tok(heads),
                  pl.BlockSpec((None, heads, chunk), lambda bi, i: (bi, 0, i)),
                  full((CONV_W, cdim)), full((1, cdim)),
                  full((1, heads)), full((heads, 1)), full((1, heads)), full((heads, 1)),
                  full((1, inner)), full((1, inner))],
        out_specs=tok(inner),
        out_shape=jax.ShapeDtypeStruct((b, s, inner), BF16),
        scratch_shapes=[pltpu.VMEM((TAIL + chunk, cdim), F32),
                        pltpu.VMEM((groups, ns, hpg * hd), F32)],
        compiler_params=_cparams("parallel", "arbitrary"),
        name="ssd_core",
    )(z, xbc, dt, dt_t, conv_w, rowv(conv_b), rowv(dt_bias), colv(dt_bias), rowv(a_log), colv(a_log),
      rowv(jnp.repeat(d_skip, hd)), rowv(norm_w))


def _ssd_layer(x, nw, sc, sh, g, w_in, conv_w, conv_b, dt_bias, a_log, d_skip, norm_w, w_out,
               groups, hd, ns):
    inner = w_out.shape[0]
    heads = inner // hd
    cdim = inner + 2 * groups * ns
    w_pad = _pad_cols(w_in, inner + cdim + LANES).astype(BF16)
    z, xbc, dt = _norm_mod_matmul(x, nw, sc, sh, w_pad, [inner, cdim, LANES], [BF16, BF16, F32])
    y = _ssd_core(z, xbc, dt[..., :heads], conv_w, conv_b, dt_bias, a_log, d_skip, norm_w,
                  groups, hd, ns)
    return _outproj_residual(x, y, g, w_out.astype(BF16))


def _rwkv_inproj_kernel(x_ref, nw_ref, sc_ref, sh_ref, mu_ref, wr_ref, wk_ref, wv_ref,
                        wdw_ref, wda_ref, wdg_ref, w0_ref, wlb_ref, a0_ref, alb_ref, glb_ref,
                        r_ref, k_ref, v_ref, lw_ref, a_ref, g_ref, last_scr):
    i = pl.program_id(1)
    tm = x_ref.shape[0]

    @pl.when(i == 0)
    def _():
        last_scr[...] = jnp.zeros_like(last_scr)

    h = _norm_mod(x_ref[...], nw_ref[...], sc_ref[...], sh_ref[...])
    row = lax.broadcasted_iota(jnp.int32, (tm, 1), 0)
    h_prev = jnp.where(row == 0, last_scr[0:1, :], pltpu.roll(h, 1, axis=0))
    last_scr[...] = jnp.broadcast_to(h[tm - 1:tm, :], last_scr.shape)
    xx = h_prev - h

    def proj(b, w_ref):
        xb = (h + xx * mu_ref[b:b + 1, :]).astype(BF16)
        return jnp.dot(xb, w_ref[...], preferred_element_type=F32)

    r_ref[...] = proj(0, wr_ref).astype(r_ref.dtype)
    k_ref[...] = proj(1, wk_ref).astype(k_ref.dtype)
    v_ref[...] = proj(2, wv_ref).astype(v_ref.dtype)
    dw = proj(3, wdw_ref)
    da = proj(4, wda_ref)
    dg = proj(5, wdg_ref)
    z = w0_ref[...] + _bdot(jnp.tanh(dw), wlb_ref[...])
    w_log = -_softplus(-z) - 0.5
    lw_ref[...] = -jnp.exp(w_log)
    a_ref[...] = _sigmoid(a0_ref[...] + _bdot(da, alb_ref[...])).astype(a_ref.dtype)
    g_ref[...] = _bdot(_sigmoid(dg), glb_ref[...]).astype(g_ref.dtype)


def _pad_rows(w, n):
    return jnp.pad(w, ((0, n - w.shape[0]), (0, 0)))


def _round_up(n, m):
    return -(-n // m) * m


def _rwkv_inproj(x, nw, sc, sh, w_in, mu, w0, w_lora_b, a0, a_lora_b, g_lora_b, tm=256):
    b, s, d = x.shape
    nl = [w_lora_b.shape[0], a_lora_b.shape[0], g_lora_b.shape[0]]
    nlp = [_round_up(n, LANES) for n in nl]
    offs = [0, d, 2 * d, 3 * d, 3 * d + nl[0], 3 * d + nl[0] + nl[1]]
    wr, wk, wv = (w_in[:, offs[j]:offs[j] + d].astype(BF16) for j in range(3))
    wl = [_pad_cols(w_in[:, offs[3 + j]:offs[3 + j] + nl[j]], nlp[j]).astype(BF16) for j in range(3)]
    lb = [_pad_rows(m, n).astype(BF16) for m, n in zip((w_lora_b, a_lora_b, g_lora_b), nlp)]
    rowv = lambda a: a.reshape(1, d).astype(F32)
    tok = pl.BlockSpec((None, tm, d), lambda bi, i: (bi, i, 0))
    vec = pl.BlockSpec((None, 1, d), lambda bi, i: (bi, 0, 0))
    full = lambda a: pl.BlockSpec(a.shape, lambda bi, i: (0, 0))
    ins = [x, rowv(nw), sc, sh, mu.astype(F32), wr, wk, wv, wl[0], wl[1], wl[2],
           rowv(w0), lb[0], rowv(a0), lb[1], lb[2]]
    specs = [tok, full(ins[1]), vec, vec] + [full(a) for a in ins[4:]]
    dts = [BF16, BF16, BF16, F32, BF16, BF16]
    return pl.pallas_call(
        _rwkv_inproj_kernel,
        grid=(b, s // tm),
        in_specs=specs,
        out_specs=[tok] * 6,
        out_shape=[jax.ShapeDtypeStruct((b, s, d), dt) for dt in dts],
        scratch_shapes=[pltpu.VMEM((8, d), F32)],
        compiler_params=_cparams("parallel", "arbitrary"),
        name="rwkv_inproj",
    )(*ins)


RW_GN_EPS = 64e-5


def _rwkv_kernel(r_ref, k_ref, v_ref, lw_ref, a_ref, g_ref, kk_ref, ka_ref, rk_ref, lnw_ref, lnb_ref,
                 y_ref, s_scr, *, hd):
    i = pl.program_id(1)
    l, d = r_ref.shape
    pair = 2 * hd
    npairs = d // pair

    @pl.when(i == 0)
    def _():
        s_scr[...] = jnp.zeros_like(s_scr)

    lw = lw_ref[...]
    cum = _cumsum_rows(lw)
    lo1 = lax.broadcasted_iota(jnp.int32, (1, pair), 1) < hd
    r2 = lax.broadcasted_iota(jnp.int32, (2 * l, 1), 0) < l
    c2 = lax.broadcasted_iota(jnp.int32, (1, pair), 1) < hd
    stack_mask = r2 == c2
    rr = lax.broadcasted_iota(jnp.int32, (2 * l, 2 * l), 0)
    cc = lax.broadcasted_iota(jnp.int32, (2 * l, 2 * l), 1)
    same = (rr < l) == (cc < l)
    strict = same & (rr > cc)
    incl = same & (rr >= cc)
    kr = lax.broadcasted_iota(jnp.int32, (pair, pair), 0) < hd
    kc = lax.broadcasted_iota(jnp.int32, (pair, pair), 1) < hd
    bd = kr == kc

    def half_sum(t):
        s_lo = jnp.sum(jnp.where(lo1, t, 0.0), axis=-1, keepdims=True)
        s_hi = jnp.sum(jnp.where(lo1, 0.0, t), axis=-1, keepdims=True)
        return jnp.where(lo1, s_lo, s_hi)

    def stack(t, masked):
        t2 = jnp.concatenate([t, t], axis=0)
        return jnp.where(stack_mask, t2, 0.0) if masked else t2

    nt = (((1,), (1,)), ((), ()))
    tn = (((0,), (0,)), ((), ()))
    pairs = range(npairs)
    sls = [slice(p * pair, (p + 1) * pair) for p in pairs]
    st = []
    for p in pairs:
        sl = sls[p]
        r = r_ref[:, sl].astype(F32)
        k = k_ref[:, sl].astype(F32)
        v = v_ref[:, sl].astype(F32)
        a = a_ref[:, sl].astype(F32)
        cm = cum[:, sl]
        gam = jnp.exp(cm)
        gam_prev = jnp.exp(cm - lw[:, sl])
        inv_gam = jnp.exp(-cm)
        kk = k * kk_ref[:, sl]
        kk = kk / jnp.maximum(jnp.sqrt(half_sum(kk * kk)), 1e-12)
        k2 = k * (1.0 + (a - 1.0) * ka_ref[:, sl])
        at = stack(-kk * gam_prev, True).astype(BF16)
        rt = stack(r * gam, True).astype(BF16)
        bt = (kk * a * inv_gam).astype(BF16)
        kt = (k2 * inv_gam).astype(BF16)
        v_st = stack(v, False)
        s0 = s_scr[p]
        lhs = jnp.concatenate([at, rt], axis=0)
        bk = jnp.concatenate([bt, bt, kt, kt], axis=0)
        big = lax.dot_general(lhs, bk, nt, preferred_element_type=F32)
        ls = lax.dot_general(lhs, s0.astype(BF16), nt, preferred_element_type=F32)
        st.append(dict(
            v_st=v_st, v_b=v_st.astype(BF16), bk=bk, s0=s0, gam_last=gam[l - 1:l, :],
            bonus=half_sum(r * k2 * rk_ref[:, sl]) * v,
            m=jnp.where(strict, big[0:2 * l, 0:2 * l], 0.0).astype(BF16),
            a_ak=jnp.where(strict, big[0:2 * l, 2 * l:4 * l], 0.0).astype(BF16),
            a_r=jnp.concatenate([jnp.where(incl, big[2 * l:4 * l, 0:2 * l], 0.0),
                                 jnp.where(incl, big[2 * l:4 * l, 2 * l:4 * l], 0.0)],
                                axis=1).astype(BF16),
            ls_a=ls[0:2 * l], ls_r=ls[2 * l:4 * l]))

    for q in st:
        q["x"] = q["ls_a"] + jnp.dot(q["a_ak"], q["v_b"], preferred_element_type=F32)
    span = 1
    while span < l:
        for q in st:
            q["x"] = q["x"] + jnp.dot(q["m"], q["x"].astype(BF16), preferred_element_type=F32)
        span *= 2
        if span < l:
            for q in st:
                q["m"] = jnp.dot(q["m"], q["m"], preferred_element_type=F32).astype(BF16)

    for p in pairs:
        q, sl = st[p], sls[p]
        x_sol = q["x"]
        y_st = q["ls_r"] + jnp.dot(q["a_r"], jnp.concatenate([x_sol.astype(BF16), q["v_b"]], axis=0),
                                   preferred_element_type=F32)
        uv = jnp.concatenate([jnp.where(stack_mask, x_sol, 0.0),
                              jnp.where(stack_mask, q["v_st"], 0.0)], axis=0).astype(BF16)
        upd = lax.dot_general(uv, q["bk"], tn, preferred_element_type=F32)
        s_scr[p] = (q["s0"] + jnp.where(bd, upd, 0.0)) * q["gam_last"]

        y = jnp.where(lo1, y_st[0:l], y_st[l:2 * l])
        mean = half_sum(y) * (1.0 / hd)
        yc = y - mean
        var = half_sum(yc * yc) * (1.0 / hd)
        y = yc * lax.rsqrt(var + RW_GN_EPS) * lnw_ref[:, sl] + lnb_ref[:, sl]
        y_ref[:, sl] = ((y + q["bonus"]) * g_ref[:, sl].astype(F32)).astype(y_ref.dtype)


def _rwkv_core(r, k, v, lw, a, g, k_k, k_a, r_k, ln_w, ln_b, hd, chunk=64):
    b, s, d = r.shape
    rowv = lambda t: t.reshape(1, d).astype(F32)
    tok = pl.BlockSpec((None, chunk, d), lambda bi, i: (bi, i, 0))
    full = pl.BlockSpec((1, d), lambda bi, i: (0, 0))
    return pl.pallas_call(
        functools.partial(_rwkv_kernel, hd=hd),
        grid=(b, s // chunk),
        in_specs=[tok] * 6 + [full] * 5,
        out_specs=tok,
        out_shape=jax.ShapeDtypeStruct((b, s, d), BF16),
        scratch_shapes=[pltpu.VMEM((d // (2 * hd), 2 * hd, 2 * hd), F32)],
        compiler_params=_cparams("parallel", "arbitrary"),
        name="rwkv_core",
    )(r, k, v, lw, a, g, rowv(k_k), rowv(k_a), rowv(r_k), rowv(ln_w), rowv(ln_b))


def _rwkv_layer(x, nw, sc, sh, g_mod, w_in, mu, w0, w_lora_b, a0, a_lora_b, g_lora_b, k_k, k_a, r_k,
                ln_w, ln_b, w_out, hd):
    r, k, v, lw, a, g = _rwkv_inproj(x, nw, sc, sh, w_in, mu, w0, w_lora_b, a0, a_lora_b, g_lora_b)
    y = _rwkv_core(r, k, v, lw, a, g, k_k, k_a, r_k, ln_w, ln_b, hd)
    return _outproj_residual(x, y, g_mod, w_out.astype(BF16))


MOE_TILE = 1024
MOE_ROWS = 128
MOE_SPECIALISED_BLOCKS = (2, 3, 4)


def _router_kernel(x_ref, nw_ref, sc_ref, sh_ref, wr_ref, hb_ref, comb_ref, pos_ref, cnt_ref, *, ne):
    t = x_ref.shape[0]
    h = _norm_mod(x_ref[...], nw_ref[...], sc_ref[...], sh_ref[...])
    hb_ref[...] = h.astype(hb_ref.dtype)
    lane = lax.broadcasted_iota(jnp.int32, (t, LANES), 1)
    logits = jnp.where(lane < ne, _dot3(h, wr_ref[...]), NEG_BIG)
    m1 = jnp.max(logits, axis=-1, keepdims=True)
    i1 = jnp.min(jnp.where(logits == m1, lane, LANES), axis=-1, keepdims=True)
    sel1 = lane == i1
    rest = jnp.where(sel1, NEG_BIG, logits)
    m2 = jnp.max(rest, axis=-1, keepdims=True)
    i2 = jnp.min(jnp.where(rest == m2, lane, LANES), axis=-1, keepdims=True)
    sel2 = lane == i2
    e2 = jnp.exp(m2 - m1)
    w1 = 1.0 / (1.0 + e2)
    comb = jnp.where(sel1, w1, 0.0) + jnp.where(sel2, e2 * w1, 0.0)
    sel = jnp.where(sel1 | sel2, 1.0, 0.0)
    below = jnp.where(_tri(t, "strict_lower"), 1.0, 0.0).astype(BF16)
    slot = jnp.dot(below, sel.astype(BF16), preferred_element_type=F32)
    comb_ref[...] = comb.T
    pos_ref[...] = jnp.where(sel > 0.0, slot, -1.0).T
    cnt = jnp.sum(sel, axis=0, keepdims=True).astype(jnp.int32)
    cnt_ref[...] = jnp.broadcast_to(cnt, cnt_ref.shape)


def _moe_router(x, nw, sc, sh, w_router, tile):
    b, s, d = x.shape
    ne = w_router.shape[1]
    nt = s // tile
    vec = pl.BlockSpec((None, 1, d), lambda bi, i: (bi, 0, 0))
    return pl.pallas_call(
        functools.partial(_router_kernel, ne=ne),
        grid=(b, nt),
        in_specs=[pl.BlockSpec((None, tile, d), lambda bi, i: (bi, i, 0)),
                  pl.BlockSpec((1, d), lambda bi, i: (0, 0)), vec, vec,
                  pl.BlockSpec((d, LANES), lambda bi, i: (0, 0))],
        out_specs=[pl.BlockSpec((None, tile, d), lambda bi, i: (bi, i, 0)),
                   pl.BlockSpec((None, LANES, tile), lambda bi, i: (bi, 0, i)),
                   pl.BlockSpec((None, LANES, tile), lambda bi, i: (bi, 0, i)),
                   pl.BlockSpec((None, None, 8, LANES), lambda bi, i: (bi, i, 0, 0))],
        out_shape=[jax.ShapeDtypeStruct((b, s, d), BF16),
                   jax.ShapeDtypeStruct((b, LANES, s), F32),
                   jax.ShapeDtypeStruct((b, LANES, s), F32),
                   jax.ShapeDtypeStruct((b, nt, 8, LANES), jnp.int32)],
        compiler_params=_cparams("parallel", "parallel"),
        name="moe_router",
    )(x, nw.reshape(1, d), sc, sh, _pad_cols(w_router, LANES))


def _expert_kernel(cnt_ref, x_ref, hb_ref, pos_ref, comb_ref, g_ref, wg_ref, wu_ref, wd_ref, o_ref,
                   hg_scr, yacc_scr, *, rows, ne):
    bi, ti, e, f = (pl.program_id(a) for a in range(4))
    nf = pl.num_programs(3)
    t = x_ref.shape[0]
    cnt = cnt_ref[(bi * pl.num_programs(1) + ti) * ne + e]
    nblk = lax.shift_right_logical(cnt + (rows - 1), int(math.log2(rows)))

    @pl.when((e == 0) & (f == 0))
    def _():
        o_ref[...] = x_ref[...]

    def run_rows(row0, m):
        rs = pl.ds(row0, m)

        def one_hot():
            slot = pos_ref[pl.ds(e, 1), :]
            want = (lax.broadcasted_iota(jnp.int32, (m, 1), 0) + row0).astype(F32)
            return slot == want

        @pl.when(f == 0)
        def _():
            p = jnp.where(one_hot(), 1.0, 0.0).astype(BF16)
            hg_scr[rs, :] = jnp.dot(p, hb_ref[...], preferred_element_type=F32).astype(BF16)

        hg = hg_scr[rs, :]
        gate = jnp.dot(hg, wg_ref[...], preferred_element_type=F32)
        up = jnp.dot(hg, wu_ref[...], preferred_element_type=F32)
        act = (gate * _sigmoid(gate) * up).astype(BF16)
        part = jnp.dot(act, wd_ref[...], preferred_element_type=F32)

        @pl.when(f == 0)
        def _():
            yacc_scr[rs, :] = part

        @pl.when(f > 0)
        def _():
            yacc_scr[rs, :] += part

        @pl.when(f == nf - 1)
        def _():
            hit = one_hot()
            wrow = comb_ref[pl.ds(e, 1), :]
            wgt = jnp.sum(jnp.where(hit, wrow, 0.0), axis=-1, keepdims=True)
            yw = (yacc_scr[rs, :] * wgt * g_ref[...]).astype(BF16)
            p = jnp.where(hit, 1.0, 0.0).astype(BF16)
            o_ref[...] += lax.dot_general(p, yw, (((0,), (0,)), ((), ())),
                                          preferred_element_type=F32)

    lo = 1
    for n in MOE_SPECIALISED_BLOCKS:
        @pl.when((nblk >= lo) & (nblk <= n))
        def _():
            run_rows(0, n * rows)
        lo = n + 1

    @pl.when(nblk >= lo)
    def _():
        def body(j, carry):
            run_rows(pl.multiple_of(j * rows, rows), rows)
            return carry
        lax.fori_loop(0, nblk, body, 0)


def _moe_experts(x, hb, pos_t, comb_t, counts, g, w_gu_bf16, w_down_bf16, tile, rows, tf=896):
    b, s, d = x.shape
    ne, dff, _ = w_down_bf16.shape
    nf = dff // tf
    nt = s // tile
    grid_spec = pltpu.PrefetchScalarGridSpec(
        num_scalar_prefetch=1,
        grid=(b, nt, ne, nf),
        in_specs=[
            pl.BlockSpec((None, tile, d), lambda bi, i, e, f, c: (bi, i, 0)),
            pl.BlockSpec((None, tile, d), lambda bi, i, e, f, c: (bi, i, 0)),
            pl.BlockSpec((None, 8, tile), lambda bi, i, e, f, c: (bi, 0, i)),
            pl.BlockSpec((None, 8, tile), lambda bi, i, e, f, c: (bi, 0, i)),
            pl.BlockSpec((None, 1, d), lambda bi, i, e, f, c: (bi, 0, 0)),
            pl.BlockSpec((None, d, tf), lambda bi, i, e, f, c: (e, 0, f)),
            pl.BlockSpec((None, d, tf), lambda bi, i, e, f, c: (e, 0, f + nf)),
            pl.BlockSpec((None, tf, d), lambda bi, i, e, f, c: (e, f, 0)),
        ],
        out_specs=pl.BlockSpec((None, tile, d), lambda bi, i, e, f, c: (bi, i, 0)),
        scratch_shapes=[pltpu.VMEM((tile, d), BF16), pltpu.VMEM((tile, d), F32)],
    )
    return pl.pallas_call(
        functools.partial(_expert_kernel, rows=rows, ne=ne),
        grid_spec=grid_spec,
        out_shape=jax.ShapeDtypeStruct((b, s, d), F32),
        input_output_aliases={1: 0},
        compiler_params=_cparams("parallel", "parallel", "arbitrary", "arbitrary"),
        name="moe_experts",
    )(counts, x, hb, pos_t, comb_t, g, w_gu_bf16, w_gu_bf16, w_down_bf16)


def _moe_layer(x, nw, sc, sh, g, w_router, w_gu, w_down):
    s = x.shape[1]
    ne = w_router.shape[1]
    tile = min(MOE_TILE, s)
    hb, comb_t, pos_t, counts = _moe_router(x, nw, sc, sh, w_router, tile)
    counts = counts[:, :, 0, :ne].reshape(-1)
    return _moe_experts(x, hb, pos_t, comb_t, counts, g, w_gu.astype(BF16), w_down.astype(BF16),
                        tile, MOE_ROWS)


def _final_norm_kernel(x_ref, w_ref, o_ref):
    x = x_ref[...]
    ms = jnp.mean(x * x, axis=-1, keepdims=True)
    o_ref[...] = x * lax.rsqrt(ms + RMS_EPS) * w_ref[...]


def _final_norm(x, w, tm=1024):
    b, s, d = x.shape
    return pl.pallas_call(
        _final_norm_kernel,
        grid=(b, s // tm),
        in_specs=[pl.BlockSpec((None, tm, d), lambda bi, i: (bi, i, 0)),
                  pl.BlockSpec((1, d), lambda bi, i: (0, 0))],
        out_specs=pl.BlockSpec((None, tm, d), lambda bi, i: (bi, i, 0)),
        out_shape=jax.ShapeDtypeStruct((b, s, d), F32),
        compiler_params=_cparams("parallel", "parallel"),
        name="final_norm",
    )(x, w.reshape(1, d))


ML_HEADS = 4
SSD_GROUPS, SSD_HEADDIM, SSD_STATE = 4, 64, 128
RW_HEADDIM = 64


def kernel(x, c, ada_w, ada_b, norm_w, final_norm_w, ml_w_in, ml_gate_b, ml_norm_w, ml_w_out, ssd_w_in, ssd_conv_w, ssd_conv_b, ssd_dt_bias, ssd_a_log, ssd_d, ssd_norm_w, ssd_w_out, rw_w_in, rw_mu, rw_w0, rw_w_lora_b, rw_a0, rw_a_lora_b, rw_g_lora_b, rw_k_k, rw_k_a, rw_r_k, rw_ln_w, rw_ln_b, rw_w_out, lru_w_in, lru_conv_w, lru_conv_b, lru_gx_w, lru_gx_b, lru_ga_w, lru_ga_b, lru_lambda, lru_w_out, ffn_w_gu, ffn_w_down, moe_router, moe_w_gu, moe_w_down):
    depth = ada_w.shape[0]
    mod = _ada_modulation(c, ada_w, ada_b)
    for layer in range(depth):
        sh_t, sc_t, g_t, sh_c, sc_c, g_c = [m[:, None, :] for m in jnp.split(mod[layer], 6, axis=-1)]
        kind, j = layer % 4, layer // 4
        nw_t, nw_c = norm_w[layer, 0], norm_w[layer, 1]
        if kind == 0:
            x = _mlstm_layer(x, nw_t, sc_t, sh_t, g_t, ml_w_in[j], ml_gate_b[j], ml_norm_w[j],
                             ml_w_out[j], ML_HEADS)
        elif kind == 1:
            x = _ssd_layer(x, nw_t, sc_t, sh_t, g_t, ssd_w_in[j], ssd_conv_w[j], ssd_conv_b[j],
                           ssd_dt_bias[j], ssd_a_log[j], ssd_d[j], ssd_norm_w[j], ssd_w_out[j],
                           SSD_GROUPS, SSD_HEADDIM, SSD_STATE)
        elif kind == 2:
            x = _rwkv_layer(x, nw_t, sc_t, sh_t, g_t, rw_w_in[j], rw_mu[j], rw_w0[j], rw_w_lora_b[j],
                            rw_a0[j], rw_a_lora_b[j], rw_g_lora_b[j], rw_k_k[j], rw_k_a[j], rw_r_k[j],
                            rw_ln_w[j], rw_ln_b[j], rw_w_out[j], RW_HEADDIM)
        else:
            x = _rglru_layer(x, nw_t, sc_t, sh_t, g_t, lru_w_in[j], lru_conv_w[j], lru_conv_b[j],
                             lru_gx_w[j], lru_gx_b[j], lru_ga_w[j], lru_ga_b[j], lru_lambda[j],
                             lru_w_out[j])
        if layer % 2 == 0:
            x = _ffn_dense(x, nw_c, sc_c, sh_c, g_c, ffn_w_gu[layer // 2].astype(BF16),
                           ffn_w_down[layer // 2].astype(BF16))
        else:
            x = _moe_layer(x, nw_c, sc_c, sh_c, g_c, moe_router[layer // 2], moe_w_gu[layer // 2],
                           moe_w_down[layer // 2])
    return _final_norm(x, final_norm_w)
```

```python
import functools
import math

import jax
import jax.numpy as jnp
from jax import lax
from jax.experimental import pallas as pl
from jax.experimental.pallas import tpu as pltpu

F32 = jnp.float32
BF16 = jnp.bfloat16

RMS_EPS = 1e-6
LANES = 128
VMEM_LIMIT_BYTES = 56 * 1024 * 1024


def _cparams(*sem):
    return pltpu.CompilerParams(dimension_semantics=sem, vmem_limit_bytes=VMEM_LIMIT_BYTES)


def _sigmoid(x):
    return 1.0 / (1.0 + jnp.exp(-x))


def _softplus(x):
    return jnp.maximum(x, 0.0) + jnp.log(1.0 + jnp.exp(-jnp.abs(x)))


def _bdot(a, b):
    return jnp.dot(a.astype(BF16), b.astype(BF16), preferred_element_type=F32)


def _split_bf16(a):
    hi = a.astype(BF16)
    lo = (a - hi.astype(F32)).astype(BF16)
    return hi, lo


def _dot3(a, b):
    a_hi, a_lo = _split_bf16(a)
    b_hi, b_lo = _split_bf16(b)
    d = functools.partial(jnp.dot, preferred_element_type=F32)
    return d(a_hi, b_hi) + d(a_lo, b_hi) + d(a_hi, b_lo)


def _norm_mod(x, nw, sc, sh):
    ms = jnp.mean(x * x, axis=-1, keepdims=True)
    y = x * lax.rsqrt(ms + RMS_EPS) * nw
    return y * (1.0 + sc) + sh


def _pad_cols(w, n):
    return jnp.pad(w, ((0, 0), (0, n - w.shape[1])))


def _ada_kernel(c_ref, w_ref, b_ref, o_ref):
    c = c_ref[...]
    cond = c * _sigmoid(c)
    o_ref[...] = _dot3(cond, w_ref[...]) + b_ref[...]


def _ada_modulation(c, ada_w, ada_b):
    depth, d, n = ada_w.shape
    b = c.shape[0]
    rows = 8
    cp = jnp.pad(c, ((0, rows - b), (0, 0)))
    tn = 1024
    out = pl.pallas_call(
        _ada_kernel,
        grid=(depth, n // tn),
        in_specs=[
            pl.BlockSpec((rows, d), lambda l, j: (0, 0)),
            pl.BlockSpec((None, d, tn), lambda l, j: (l, 0, j)),
            pl.BlockSpec((None, 1, tn), lambda l, j: (l, 0, j)),
        ],
        out_specs=pl.BlockSpec((None, rows, tn), lambda l, j: (l, 0, j)),
        out_shape=jax.ShapeDtypeStruct((depth, rows, n), F32),
        compiler_params=_cparams("parallel", "parallel"),
        name="ada_modulation",
    )(cp, ada_w, ada_b.reshape(depth, 1, n))
    return out[:, :b]


def _nmm_kernel(x_ref, nw_ref, sc_ref, sh_ref, w_ref, *o_refs, segs):
    hb = _norm_mod(x_ref[...], nw_ref[...], sc_ref[...], sh_ref[...]).astype(BF16)
    off = 0
    for o_ref, n in zip(o_refs, segs):
        o_ref[...] = jnp.dot(hb, w_ref[:, off:off + n],
                             preferred_element_type=F32).astype(o_ref.dtype)
        off += n


def _norm_mod_matmul(x, nw, sc, sh, w_bf16, segs, dtypes, tm=512):
    b, s, d = x.shape
    ntot = w_bf16.shape[1]
    assert sum(segs) == ntot and s % tm == 0
    vec = pl.BlockSpec((None, 1, d), lambda bi, i: (bi, 0, 0))
    return pl.pallas_call(
        functools.partial(_nmm_kernel, segs=tuple(segs)),
        grid=(b, s // tm),
        in_specs=[
            pl.BlockSpec((None, tm, d), lambda bi, i: (bi, i, 0)),
            pl.BlockSpec((1, d), lambda bi, i: (0, 0)),
            vec, vec,
            pl.BlockSpec((d, ntot), lambda bi, i: (0, 0)),
        ],
        out_specs=[pl.BlockSpec((None, tm, n), lambda bi, i: (bi, i, 0)) for n in segs],
        out_shape=[jax.ShapeDtypeStruct((b, s, n), dt) for n, dt in zip(segs, dtypes)],
        compiler_params=_cparams("parallel", "parallel"),
        name="norm_mod_matmul",
    )(x, nw.reshape(1, d), sc, sh, w_bf16)


def _outproj_kernel(x_ref, y_ref, g_ref, w_ref, o_ref):
    o_ref[...] = x_ref[...] + g_ref[...] * jnp.dot(
        y_ref[...], w_ref[...], preferred_element_type=F32)


def _outproj_residual(x, y, g, w_bf16, tm=512):
    b, s, d = x.shape
    k = y.shape[-1]
    return pl.pallas_call(
        _outproj_kernel,
        grid=(b, s // tm),
        in_specs=[
            pl.BlockSpec((None, tm, d), lambda bi, i: (bi, i, 0)),
            pl.BlockSpec((None, tm, k), lambda bi, i: (bi, i, 0)),
            pl.BlockSpec((None, 1, d), lambda bi, i: (bi, 0, 0)),
            pl.BlockSpec((k, d), lambda bi, i: (0, 0)),
        ],
        out_specs=pl.BlockSpec((None, tm, d), lambda bi, i: (bi, i, 0)),
        out_shape=jax.ShapeDtypeStruct((b, s, d), F32),
        input_output_aliases={0: 0},
        compiler_params=_cparams("parallel", "parallel"),
        name="outproj_residual",
    )(x, y, g, w_bf16)


def _ffn_kernel(x_ref, nw_ref, sc_ref, sh_ref, g_ref, wg_ref, wu_ref, wd_ref, o_ref,
                h_scr, acc_scr):
    f = pl.program_id(2)

    @pl.when(f == 0)
    def _():
        h_scr[...] = _norm_mod(x_ref[...], nw_ref[...], sc_ref[...], sh_ref[...]).astype(BF16)
        acc_scr[...] = jnp.zeros_like(acc_scr)

    hb = h_scr[...]
    gate = jnp.dot(hb, wg_ref[...], preferred_element_type=F32)
    up = jnp.dot(hb, wu_ref[...], preferred_element_type=F32)
    act = (gate * _sigmoid(gate) * up).astype(BF16)
    acc_scr[...] += jnp.dot(act, wd_ref[...], preferred_element_type=F32)

    @pl.when(f == pl.num_programs(2) - 1)
    def _():
        o_ref[...] = x_ref[...] + g_ref[...] * acc_scr[...]


def _ffn_dense(x, nw, sc, sh, g, w_gu_bf16, w_down_bf16, tm=1024, tf=512):
    b, s, d = x.shape
    dff = w_down_bf16.shape[0]
    nf = dff // tf
    vec = pl.BlockSpec((None, 1, d), lambda bi, i, f: (bi, 0, 0))
    return pl.pallas_call(
        _ffn_kernel,
        grid=(b, s // tm, nf),
        in_specs=[
            pl.BlockSpec((None, tm, d), lambda bi, i, f: (bi, i, 0)),
            pl.BlockSpec((1, d), lambda bi, i, f: (0, 0)),
            vec, vec, vec,
            pl.BlockSpec((d, tf), lambda bi, i, f: (0, f)),
            pl.BlockSpec((d, tf), lambda bi, i, f: (0, f + nf)),
            pl.BlockSpec((tf, d), lambda bi, i, f: (f, 0)),
        ],
        out_specs=pl.BlockSpec((None, tm, d), lambda bi, i, f: (bi, i, 0)),
        out_shape=jax.ShapeDtypeStruct((b, s, d), F32),
        scratch_shapes=[pltpu.VMEM((tm, d), BF16), pltpu.VMEM((tm, d), F32)],
        input_output_aliases={0: 0},
        compiler_params=_cparams("parallel", "parallel", "arbitrary"),
        name="ffn_dense",
    )(x, nw.reshape(1, d), sc, sh, g, w_gu_bf16, w_gu_bf16, w_down_bf16)


CONV_W = 4
TAIL = 8


def _causal_conv(pad_ref, x, w_ref, b_ref, first):
    l = x.shape[0]

    @pl.when(first)
    def _():
        pad_ref[0:TAIL, :] = jnp.zeros((TAIL, x.shape[1]), F32)

    pad_ref[TAIL:2 * TAIL, :] = x[0:TAIL, :]
    y = b_ref[...] + w_ref[CONV_W - 1:CONV_W, :] * x
    head = y[0:TAIL, :]
    for k in range(CONV_W - 1):
        shift = CONV_W - 1 - k
        y = y + w_ref[k:k + 1, :] * pltpu.roll(x, shift, axis=0)
        head = head + w_ref[k:k + 1, :] * pad_ref[TAIL - shift:2 * TAIL - shift, :]
    pad_ref[0:TAIL, :] = x[l - TAIL:, :]
    return jnp.concatenate([head, y[TAIL:, :]], axis=0)


LRU_C = 8.0


def _gelu_tanh(x):
    return 0.5 * x * (1.0 + jnp.tanh(math.sqrt(2.0 / math.pi) * (x + 0.044715 * (x * x * x))))


def _lru_kernel(xbr_ref, ybr_ref, cw_ref, cb_ref, gxw_ref, gxb_ref, gaw_ref, gab_ref, lam_ref,
                o_ref, pad_scr, h_scr):
    i = pl.program_id(1)
    l, w = xbr_ref.shape
    nblk, blk = gxw_ref.shape[0], gxw_ref.shape[1]

    @pl.when(i == 0)
    def _():
        h_scr[...] = jnp.zeros_like(h_scr)

    xb = _causal_conv(pad_scr, xbr_ref[...], cw_ref, cb_ref, i == 0)
    xbb = xb.astype(BF16)
    gx = jnp.concatenate([jnp.dot(xbb[:, n * blk:(n + 1) * blk], gxw_ref[n],
                                  preferred_element_type=F32) for n in range(nblk)], axis=-1)
    ga = jnp.concatenate([jnp.dot(xbb[:, n * blk:(n + 1) * blk], gaw_ref[n],
                                  preferred_element_type=F32) for n in range(nblk)], axis=-1)
    gate_x = _sigmoid(gx + gxb_ref[...])
    gate_a = _sigmoid(ga + gab_ref[...])
    log_a = LRU_C * gate_a * (-_softplus(-lam_ref[...]))
    a = jnp.exp(log_a)
    u = jnp.sqrt(1.0 - jnp.exp(2.0 * log_a)) * gate_x * xb

    row = lax.broadcasted_iota(jnp.int32, (l, 1), 0)
    d = 1
    while d < l:
        if d < TAIL:
            a_sh = pltpu.roll(a, d, axis=0)
            u_sh = pltpu.roll(u, d, axis=0)
            valid = row >= d
            u = jnp.where(valid, a * u_sh + u, u)
            a = jnp.where(valid, a * a_sh, a)
        else:
            u = jnp.concatenate([u[:d], a[d:] * u[:l - d] + u[d:]], axis=0)
            a = jnp.concatenate([a[:d], a[d:] * a[:l - d]], axis=0)
        d *= 2
    hs = a * h_scr[0:1, :] + u
    h_scr[...] = jnp.broadcast_to(hs[l - 1:l, :], h_scr.shape)
    o_ref[...] = (_gelu_tanh(ybr_ref[...].astype(F32)) * hs).astype(o_ref.dtype)


def _rglru_core(x_br, y_br, conv_w, conv_b, gx_w, gx_b, ga_w, ga_b, lam, chunk=256):
    b, s, w = x_br.shape
    nblk, blk, _ = gx_w.shape
    row = lambda a: a.reshape(1, w).astype(F32)
    full2 = lambda shape: pl.BlockSpec(shape, lambda bi, i: (0, 0))
    full3 = lambda shape: pl.BlockSpec(shape, lambda bi, i: (0, 0, 0))
    return pl.pallas_call(
        _lru_kernel,
        grid=(b, s // chunk),
        in_specs=[
            pl.BlockSpec((None, chunk, w), lambda bi, i: (bi, i, 0)),
            pl.BlockSpec((None, chunk, w), lambda bi, i: (bi, i, 0)),
            full2((CONV_W, w)), full2((1, w)),
            full3((nblk, blk, blk)), full2((1, w)),
            full3((nblk, blk, blk)), full2((1, w)),
            full2((1, w)),
        ],
        out_specs=pl.BlockSpec((None, chunk, w), lambda bi, i: (bi, i, 0)),
        out_shape=jax.ShapeDtypeStruct((b, s, w), BF16),
        scratch_shapes=[pltpu.VMEM((2 * TAIL, w), F32), pltpu.VMEM((8, w), F32)],
        compiler_params=_cparams("parallel", "arbitrary"),
        name="rglru_core",
    )(x_br, y_br, conv_w, row(conv_b), gx_w.astype(BF16), row(gx_b), ga_w.astype(BF16), row(ga_b),
      row(lam))


def _rglru_layer(x, nw, sc, sh, g, w_in, conv_w, conv_b, gx_w, gx_b, ga_w, ga_b, lam, w_out):
    w = w_in.shape[1] // 2
    y_br, x_br = _norm_mod_matmul(x, nw, sc, sh, w_in.astype(BF16), [w, w], [BF16, F32])
    out = _rglru_core(x_br, y_br, conv_w, conv_b, gx_w, gx_b, ga_w, ga_b, lam)
    return _outproj_residual(x, out, g, w_out.astype(BF16))


NEG_BIG = -1e30


def _split3(a):
    hi = a.astype(BF16)
    r = a - hi.astype(F32)
    mid = r.astype(BF16)
    lo = (r - mid.astype(F32)).astype(BF16)
    return hi, mid, lo


def _tri(l, kind):
    r = lax.broadcasted_iota(jnp.int32, (l, l), 0)
    c = lax.broadcasted_iota(jnp.int32, (l, l), 1)
    m = {"lower": r >= c, "strict_lower": r > c, "upper": r <= c}[kind]
    return m


def _cumsum_rows(a):
    t = jnp.where(_tri(a.shape[0], "lower"), 1.0, 0.0).astype(BF16)
    return sum(jnp.dot(t, p, preferred_element_type=F32) for p in _split3(a))


def _cumsum_lanes(a):
    t = jnp.where(_tri(a.shape[1], "upper"), 1.0, 0.0).astype(BF16)
    return sum(jnp.dot(p, t, preferred_element_type=F32) for p in _split3(a))


def _mlstm_kernel(q_ref, k_ref, v_ref, o_ref, gc_ref, gr_ref, gbr_ref, gbc_ref, nw_ref, y_ref,
                  c_scr, n_scr, m_scr, *, heads, dk, dv):
    i = pl.program_id(1)
    l = q_ref.shape[0]

    @pl.when(i == 0)
    def _():
        c_scr[...] = jnp.zeros_like(c_scr)
        n_scr[...] = jnp.zeros_like(n_scr)
        m_scr[...] = jnp.zeros_like(m_scr)

    gc = gc_ref[...] + gbr_ref[...]
    gr = gr_ref[...] + gbc_ref[...]
    li_col = gc[:, 0:heads]
    li_row = gr[0:heads, :]
    g_col = _cumsum_rows(-_softplus(-gc))[:, heads:2 * heads]
    g_row = _cumsum_lanes(-_softplus(-gr))[heads:2 * heads, :]
    causal = _tri(l, "lower")
    scale = dk ** -0.5

    for h in range(heads):
        a_col = g_col[:, h:h + 1]
        b_row = li_row[h:h + 1, :] - g_row[h:h + 1, :]
        b_col = li_col[:, h:h + 1] - a_col
        m_prev = m_scr[h, 0:1, 0:1]
        logd = jnp.where(causal, a_col + b_row, NEG_BIG)
        log_inter = a_col + m_prev
        m_row = jnp.maximum(jnp.max(logd, axis=-1, keepdims=True), log_inter)
        dmat = jnp.exp(logd - m_row)
        e_inter = jnp.exp(log_inter - m_row)

        qs = (q_ref[:, h * dk:(h + 1) * dk].astype(F32) * scale).astype(BF16)
        k = k_ref[:, h * dk:(h + 1) * dk]
        v = v_ref[:, h * dv:(h + 1) * dv]
        s = lax.dot_general(qs, k, (((1,), (1,)), ((), ())), preferred_element_type=F32) * dmat
        c_mat = c_scr[h]
        n_vec = n_scr[h, 0:1, :]
        num = (jnp.dot(s.astype(BF16), v, preferred_element_type=F32)
               + e_inter * jnp.dot(qs, c_mat.astype(BF16), preferred_element_type=F32))
        den = (jnp.sum(s, axis=-1, keepdims=True)
               + e_inter * jnp.sum(qs.astype(F32) * n_vec, axis=-1, keepdims=True))
        denom = jnp.maximum(jnp.abs(den), jnp.exp(-m_row))
        hh = num / denom
        hh = hh * lax.rsqrt(jnp.mean(hh * hh, axis=-1, keepdims=True) + RMS_EPS)
        gate = _sigmoid(o_ref[:, h * dv:(h + 1) * dv].astype(F32))
        y_ref[:, h * dv:(h + 1) * dv] = (hh * nw_ref[:, h * dv:(h + 1) * dv] * gate).astype(y_ref.dtype)

        g_last = a_col[l - 1:l, :]
        log_w = g_last + b_col
        m_new = jnp.maximum(g_last + m_prev, jnp.max(log_w, axis=0, keepdims=True))
        kw = k.astype(F32) * jnp.exp(log_w - m_new)
        decay = jnp.exp(g_last + m_prev - m_new)
        c_scr[h] = decay * c_mat + lax.dot_general(
            kw.astype(BF16), v, (((0,), (0,)), ((), ())), preferred_element_type=F32)
        n_scr[h, 0:1, :] = decay * n_vec + jnp.sum(kw, axis=0, keepdims=True)
        m_scr[h] = jnp.broadcast_to(m_new, m_scr.shape[1:])


def _mlstm_core(q, k, v, o, gates, gate_b, norm_w, heads, chunk=128):
    b, s, hk = q.shape
    hv = v.shape[-1]
    dk, dv = hk // heads, hv // heads
    g2 = 2 * heads
    gates_t = jnp.swapaxes(gates, 1, 2)
    gb = gate_b.reshape(1, g2).astype(F32)
    tok = lambda n: pl.BlockSpec((None, chunk, n), lambda bi, i: (bi, i, 0))
    full = lambda shape: pl.BlockSpec(shape, lambda bi, i: (0, 0))
    return pl.pallas_call(
        functools.partial(_mlstm_kernel, heads=heads, dk=dk, dv=dv),
        grid=(b, s // chunk),
        in_specs=[tok(hk), tok(hk), tok(hv), tok(hv), tok(g2),
                  pl.BlockSpec((None, g2, chunk), lambda bi, i: (bi, 0, i)),
                  full((1, g2)), full((g2, 1)), full((1, hv))],
        out_specs=tok(hv),
        out_shape=jax.ShapeDtypeStruct((b, s, hv), BF16),
        scratch_shapes=[pltpu.VMEM((heads, dk, dv), F32), pltpu.VMEM((heads, 8, dk), F32),
                        pltpu.VMEM((heads, 8, LANES), F32)],
        compiler_params=_cparams("parallel", "arbitrary"),
        name="mlstm_core",
    )(q, k, v, o, gates, gates_t, gb, gb.reshape(g2, 1), norm_w.reshape(1, hv).astype(F32))


def _mlstm_layer(x, nw, sc, sh, g, w_in, gate_b, norm_w, w_out, heads):
    hv = w_out.shape[0]
    hk = (w_in.shape[1] - 2 * hv - 2 * heads) // 2
    w_pad = _pad_cols(w_in, 2 * hk + 2 * hv + LANES).astype(BF16)
    q, k, v, o, gates = _norm_mod_matmul(x, nw, sc, sh, w_pad, [hk, hk, hv, hv, LANES],
                                         [BF16, BF16, BF16, BF16, F32])
    y = _mlstm_core(q, k, v, o, gates[..., :2 * heads], gate_b, norm_w, heads)
    return _outproj_residual(x, y, g, w_out.astype(BF16))


def _pair_select(lo_half, a, b):
    return jnp.where(lo_half, a, b)


def _ssd_kernel(z_ref, xbc_ref, dtc_ref, dtr_ref, cw_ref, cb_ref, dbr_ref, dbc_ref, alr_ref, alc_ref,
                dsk_ref, nw_ref, y_ref, pad_scr, st_scr, *, groups, hpg, hd, ns):
    i = pl.program_id(1)
    l = z_ref.shape[0]
    inner = groups * hpg * hd
    gw = hpg * hd
    pair = 2 * hd

    @pl.when(i == 0)
    def _():
        st_scr[...] = jnp.zeros_like(st_scr)

    conv = _causal_conv(pad_scr, xbc_ref[...].astype(F32), cw_ref, cb_ref, i == 0)
    conv = conv * _sigmoid(conv)
    xs = conv[:, 0:inner]
    xs_b = xs.astype(BF16)
    bm = conv[:, inner:inner + groups * ns].astype(BF16)
    cm = conv[:, inner + groups * ns:inner + 2 * groups * ns].astype(BF16)

    dt_col = _softplus(dtc_ref[...] + dbr_ref[...])
    dt_row = _softplus(dtr_ref[...] + dbc_ref[...])
    acum_col = _cumsum_rows(dt_col * (-jnp.exp(alr_ref[...])))
    acum_row = _cumsum_lanes(dt_row * (-jnp.exp(alc_ref[...])))
    a_last = acum_col[l - 1:l, :]
    ea_col = jnp.exp(acum_col)
    ws_col = jnp.exp(a_last - acum_col) * dt_col
    ea_last = jnp.exp(a_last)

    causal = _tri(l, "lower")
    lo_half = lax.broadcasted_iota(jnp.int32, (1, pair), 1) < hd

    for g in range(groups):
        bg = bm[:, g * ns:(g + 1) * ns]
        cg = cm[:, g * ns:(g + 1) * ns]
        cb = lax.dot_general(cg, bg, (((1,), (1,)), ((), ())), preferred_element_type=F32)
        st = st_scr[g]
        y_inter = jnp.dot(cg, st.astype(BF16), preferred_element_type=F32)
        xw_parts, y_parts, dec_parts = [], [], []
        for p in range(hpg // 2):
            h0 = g * hpg + 2 * p
            c0 = g * gw + p * pair
            xp = xs_b[:, c0:c0 + pair]
            ys = []
            for h in (h0, h0 + 1):
                dec = jnp.exp(jnp.where(causal, acum_col[:, h:h + 1] - acum_row[h:h + 1, :], NEG_BIG))
                wts = (cb * dec * dt_row[h:h + 1, :]).astype(BF16)
                ys.append(jnp.dot(wts, xp, preferred_element_type=F32))
            sel = lambda a: _pair_select(lo_half, a[:, h0:h0 + 1], a[:, h0 + 1:h0 + 2])
            y_parts.append(_pair_select(lo_half, ys[0], ys[1])
                           + sel(ea_col) * y_inter[:, p * pair:(p + 1) * pair])
            xw_parts.append((xs[:, c0:c0 + pair] * sel(ws_col)).astype(BF16))
            dec_parts.append(sel(ea_last))
        xw = jnp.concatenate(xw_parts, axis=-1)
        st_scr[g] = jnp.concatenate(dec_parts, axis=-1) * st + lax.dot_general(
            bg, xw, (((0,), (0,)), ((), ())), preferred_element_type=F32)
        yg = jnp.concatenate(y_parts, axis=-1)
        sl = slice(g * gw, (g + 1) * gw)
        yg = yg + dsk_ref[:, sl] * xs[:, sl]
        zg = z_ref[:, sl].astype(F32)
        yg = yg * (zg * _sigmoid(zg))
        yg = yg * lax.rsqrt(jnp.mean(yg * yg, axis=-1, keepdims=True) + RMS_EPS)
        y_ref[:, sl] = (yg * nw_ref[:, sl]).astype(y_ref.dtype)


def _ssd_core(z, xbc, dt, conv_w, conv_b, dt_bias, a_log, d_skip, norm_w, groups, hd, ns, chunk=128):
    b, s, inner = z.shape
    heads = dt.shape[-1]
    hpg = heads // groups
    cdim = xbc.shape[-1]
    dt_t = jnp.swapaxes(dt, 1, 2)
    rowv = lambda a: a.reshape(1, -1).astype(F32)
    colv = lambda a: a.reshape(-1, 1).astype(F32)
    tok = lambda n: pl.BlockSpec((None, chunk, n), lambda bi, i: (bi, i, 0))
    full = lambda shape: pl.BlockSpec(shape, lambda bi, i: (0, 0))
    return pl.pallas_call(
        functools.partial(_ssd_kernel, groups=groups, hpg=hpg, hd=hd, ns=ns),
        grid=(b, s // chunk),
        in_specs=[tok(inner), tok(cdim), tok(heads),
                  pl.BlockSpec((None, heads, chunk), lambda bi, i: (bi, 0, i)),
                  full((CONV_W, cdim)), full((1, cdim)),
                  full((1, heads)), full((heads, 1)), full((1, heads)), full((heads, 1)),
                  full((1, inner)), full((1, inner))],
        out_specs=tok(inner),
        out_shape=jax.ShapeDtypeStruct((b, s, inner), BF16),
        scratch_shapes=[pltpu.VMEM((2 * TAIL, cdim), F32),
                        pltpu.VMEM((groups, ns, hpg * hd), F32)],
        compiler_params=_cparams("parallel", "arbitrary"),
        name="ssd_core",
    )(z, xbc, dt, dt_t, conv_w, rowv(conv_b), rowv(dt_bias), colv(dt_bias), rowv(a_log), colv(a_log),
      rowv(jnp.repeat(d_skip, hd)), rowv(norm_w))


def _ssd_layer(x, nw, sc, sh, g, w_in, conv_w, conv_b, dt_bias, a_log, d_skip, norm_w, w_out,
               groups, hd, ns):
    inner = w_out.shape[0]
    heads = inner // hd
    cdim = inner + 2 * groups * ns
    w_pad = _pad_cols(w_in, inner + cdim + LANES).astype(BF16)
    z, xbc, dt = _norm_mod_matmul(x, nw, sc, sh, w_pad, [inner, cdim, LANES], [BF16, BF16, F32])
    y = _ssd_core(z, xbc, dt[..., :heads], conv_w, conv_b, dt_bias, a_log, d_skip, norm_w,
                  groups, hd, ns)
    return _outproj_residual(x, y, g, w_out.astype(BF16))


def _rwkv_inproj_kernel(x_ref, nw_ref, sc_ref, sh_ref, mu_ref, wr_ref, wk_ref, wv_ref,
                        wdw_ref, wda_ref, wdg_ref, w0_ref, wlb_ref, a0_ref, alb_ref, glb_ref,
                        r_ref, k_ref, v_ref, lw_ref, a_ref, g_ref, last_scr):
    i = pl.program_id(1)
    tm = x_ref.shape[0]

    @pl.when(i == 0)
    def _():
        last_scr[...] = jnp.zeros_like(last_scr)

    h = _norm_mod(x_ref[...], nw_ref[...], sc_ref[...], sh_ref[...])
    row = lax.broadcasted_iota(jnp.int32, (tm, 1), 0)
    h_prev = jnp.where(row == 0, last_scr[0:1, :], pltpu.roll(h, 1, axis=0))
    last_scr[...] = jnp.broadcast_to(h[tm - 1:tm, :], last_scr.shape)
    xx = h_prev - h

    def proj(b, w_ref):
        xb = (h + xx * mu_ref[b:b + 1, :]).astype(BF16)
        return jnp.dot(xb, w_ref[...], preferred_element_type=F32)

    r_ref[...] = proj(0, wr_ref).astype(r_ref.dtype)
    k_ref[...] = proj(1, wk_ref).astype(k_ref.dtype)
    v_ref[...] = proj(2, wv_ref).astype(v_ref.dtype)
    dw = proj(3, wdw_ref)
    da = proj(4, wda_ref)
    dg = proj(5, wdg_ref)
    z = w0_ref[...] + _bdot(jnp.tanh(dw), wlb_ref[...])
    w_log = -_softplus(-z) - 0.5
    lw_ref[...] = -jnp.exp(w_log)
    a_ref[...] = _sigmoid(a0_ref[...] + _bdot(da, alb_ref[...])).astype(a_ref.dtype)
    g_ref[...] = _bdot(_sigmoid(dg), glb_ref[...]).astype(g_ref.dtype)


def _pad_rows(w, n):
    return jnp.pad(w, ((0, n - w.shape[0]), (0, 0)))


def _round_up(n, m):
    return -(-n // m) * m


def _rwkv_inproj(x, nw, sc, sh, w_in, mu, w0, w_lora_b, a0, a_lora_b, g_lora_b, tm=256):
    b, s, d = x.shape
    nl = [w_lora_b.shape[0], a_lora_b.shape[0], g_lora_b.shape[0]]
    nlp = [_round_up(n, LANES) for n in nl]
    offs = [0, d, 2 * d, 3 * d, 3 * d + nl[0], 3 * d + nl[0] + nl[1]]
    wr, wk, wv = (w_in[:, offs[j]:offs[j] + d].astype(BF16) for j in range(3))
    wl = [_pad_cols(w_in[:, offs[3 + j]:offs[3 + j] + nl[j]], nlp[j]).astype(BF16) for j in range(3)]
    lb = [_pad_rows(m, n).astype(BF16) for m, n in zip((w_lora_b, a_lora_b, g_lora_b), nlp)]
    rowv = lambda a: a.reshape(1, d).astype(F32)
    tok = pl.BlockSpec((None, tm, d), lambda bi, i: (bi, i, 0))
    vec = pl.BlockSpec((None, 1, d), lambda bi, i: (bi, 0, 0))
    full = lambda a: pl.BlockSpec(a.shape, lambda bi, i: (0, 0))
    ins = [x, rowv(nw), sc, sh, mu.astype(F32), wr, wk, wv, wl[0], wl[1], wl[2],
           rowv(w0), lb[0], rowv(a0), lb[1], lb[2]]
    specs = [tok, full(ins[1]), vec, vec] + [full(a) for a in ins[4:]]
    dts = [BF16, BF16, BF16, F32, BF16, BF16]
    return pl.pallas_call(
        _rwkv_inproj_kernel,
        grid=(b, s // tm),
        in_specs=specs,
        out_specs=[tok] * 6,
        out_shape=[jax.ShapeDtypeStruct((b, s, d), dt) for dt in dts],
        scratch_shapes=[pltpu.VMEM((8, d), F32)],
        compiler_params=_cparams("parallel", "arbitrary"),
        name="rwkv_inproj",
    )(*ins)


RW_GN_EPS = 64e-5


def _rwkv_kernel(r_ref, k_ref, v_ref, lw_ref, a_ref, g_ref, kk_ref, ka_ref, rk_ref, lnw_ref, lnb_ref,
                 y_ref, s_scr, *, hd):
    i = pl.program_id(1)
    l, d = r_ref.shape
    pair = 2 * hd
    npairs = d // pair

    @pl.when(i == 0)
    def _():
        s_scr[...] = jnp.zeros_like(s_scr)

    lw = lw_ref[...]
    cum = _cumsum_rows(lw)
    lo1 = lax.broadcasted_iota(jnp.int32, (1, pair), 1) < hd
    r2 = lax.broadcasted_iota(jnp.int32, (2 * l, 1), 0) < l
    c2 = lax.broadcasted_iota(jnp.int32, (1, pair), 1) < hd
    stack_mask = r2 == c2
    rr = lax.broadcasted_iota(jnp.int32, (2 * l, 2 * l), 0)
    cc = lax.broadcasted_iota(jnp.int32, (2 * l, 2 * l), 1)
    same = (rr < l) == (cc < l)
    strict = same & (rr > cc)
    incl = same & (rr >= cc)
    kr = lax.broadcasted_iota(jnp.int32, (pair, pair), 0) < hd
    kc = lax.broadcasted_iota(jnp.int32, (pair, pair), 1) < hd
    bd = kr == kc

    def half_sum(t):
        s_lo = jnp.sum(jnp.where(lo1, t, 0.0), axis=-1, keepdims=True)
        s_hi = jnp.sum(jnp.where(lo1, 0.0, t), axis=-1, keepdims=True)
        return jnp.where(lo1, s_lo, s_hi)

    def stack(t, masked):
        t2 = jnp.concatenate([t, t], axis=0)
        return jnp.where(stack_mask, t2, 0.0) if masked else t2

    nt = (((1,), (1,)), ((), ()))
    tn = (((0,), (0,)), ((), ()))
    pairs = range(npairs)
    sls = [slice(p * pair, (p + 1) * pair) for p in pairs]
    st = []
    for p in pairs:
        sl = sls[p]
        r = r_ref[:, sl].astype(F32)
        k = k_ref[:, sl].astype(F32)
        v = v_ref[:, sl].astype(F32)
        a = a_ref[:, sl].astype(F32)
        cm = cum[:, sl]
        gam = jnp.exp(cm)
        gam_prev = jnp.exp(cm - lw[:, sl])
        inv_gam = jnp.exp(-cm)
        kk = k * kk_ref[:, sl]
        kk = kk / jnp.maximum(jnp.sqrt(half_sum(kk * kk)), 1e-12)
        k2 = k * (1.0 + (a - 1.0) * ka_ref[:, sl])
        at = stack(-kk * gam_prev, True).astype(BF16)
        rt = stack(r * gam, True).astype(BF16)
        bt = (kk * a * inv_gam).astype(BF16)
        kt = (k2 * inv_gam).astype(BF16)
        v_st = stack(v, False)
        s0 = s_scr[p]
        lhs = jnp.concatenate([at, rt], axis=0)
        bk = jnp.concatenate([bt, bt, kt, kt], axis=0)
        big = lax.dot_general(lhs, bk, nt, preferred_element_type=F32)
        ls = lax.dot_general(lhs, s0.astype(BF16), nt, preferred_element_type=F32)
        st.append(dict(
            v_st=v_st, v_b=v_st.astype(BF16), bk=bk, s0=s0, gam_last=gam[l - 1:l, :],
            bonus=half_sum(r * k2 * rk_ref[:, sl]) * v,
            m=jnp.where(strict, big[0:2 * l, 0:2 * l], 0.0).astype(BF16),
            a_ak=jnp.where(strict, big[0:2 * l, 2 * l:4 * l], 0.0).astype(BF16),
            a_r=jnp.concatenate([jnp.where(incl, big[2 * l:4 * l, 0:2 * l], 0.0),
                                 jnp.where(incl, big[2 * l:4 * l, 2 * l:4 * l], 0.0)],
                                axis=1).astype(BF16),
            ls_a=ls[0:2 * l], ls_r=ls[2 * l:4 * l]))

    for q in st:
        q["x"] = q["ls_a"] + jnp.dot(q["a_ak"], q["v_b"], preferred_element_type=F32)
    span = 1
    while span < l:
        for q in st:
            q["x"] = q["x"] + jnp.dot(q["m"], q["x"].astype(BF16), preferred_element_type=F32)
        span *= 2
        if span < l:
            for q in st:
                q["m"] = jnp.dot(q["m"], q["m"], preferred_element_type=F32).astype(BF16)

    for p in pairs:
        q, sl = st[p], sls[p]
        x_sol = q["x"]
        y_st = q["ls_r"] + jnp.dot(q["a_r"], jnp.concatenate([x_sol.astype(BF16), q["v_b"]], axis=0),
                                   preferred_element_type=F32)
        uv = jnp.concatenate([jnp.where(stack_mask, x_sol, 0.0),
                              jnp.where(stack_mask, q["v_st"], 0.0)], axis=0).astype(BF16)
        upd = lax.dot_general(uv, q["bk"], tn, preferred_element_type=F32)
        s_scr[p] = (q["s0"] + jnp.where(bd, upd, 0.0)) * q["gam_last"]

        y = jnp.where(lo1, y_st[0:l], y_st[l:2 * l])
        mean = half_sum(y) * (1.0 / hd)
        yc = y - mean
        var = half_sum(yc * yc) * (1.0 / hd)
        y = yc * lax.rsqrt(var + RW_GN_EPS) * lnw_ref[:, sl] + lnb_ref[:, sl]
        y_ref[:, sl] = ((y + q["bonus"]) * g_ref[:, sl].astype(F32)).astype(y_ref.dtype)


def _rwkv_core(r, k, v, lw, a, g, k_k, k_a, r_k, ln_w, ln_b, hd, chunk=64):
    b, s, d = r.shape
    rowv = lambda t: t.reshape(1, d).astype(F32)
    tok = pl.BlockSpec((None, chunk, d), lambda bi, i: (bi, i, 0))
    full = pl.BlockSpec((1, d), lambda bi, i: (0, 0))
    return pl.pallas_call(
        functools.partial(_rwkv_kernel, hd=hd),
        grid=(b, s // chunk),
        in_specs=[tok] * 6 + [full] * 5,
        out_specs=tok,
        out_shape=jax.ShapeDtypeStruct((b, s, d), BF16),
        scratch_shapes=[pltpu.VMEM((d // (2 * hd), 2 * hd, 2 * hd), F32)],
        compiler_params=_cparams("parallel", "arbitrary"),
        name="rwkv_core",
    )(r, k, v, lw, a, g, rowv(k_k), rowv(k_a), rowv(r_k), rowv(ln_w), rowv(ln_b))


def _rwkv_layer(x, nw, sc, sh, g_mod, w_in, mu, w0, w_lora_b, a0, a_lora_b, g_lora_b, k_k, k_a, r_k,
                ln_w, ln_b, w_out, hd):
    r, k, v, lw, a, g = _rwkv_inproj(x, nw, sc, sh, w_in, mu, w0, w_lora_b, a0, a_lora_b, g_lora_b)
    y = _rwkv_core(r, k, v, lw, a, g, k_k, k_a, r_k, ln_w, ln_b, hd)
    return _outproj_residual(x, y, g_mod, w_out.astype(BF16))


MOE_TILE = 1024
MOE_ROWS = 128
MOE_SPECIALISED_ROWS = (192, 256, 320, 384)
MOE_GROUP = 2


def _router_kernel(x_ref, nw_ref, sc_ref, sh_ref, wr_ref, hb_ref, comb_ref, pos_ref, cnt_ref, *, ne):
    t = x_ref.shape[0]
    h = _norm_mod(x_ref[...], nw_ref[...], sc_ref[...], sh_ref[...])
    hb_ref[...] = h.astype(hb_ref.dtype)
    lane = lax.broadcasted_iota(jnp.int32, (t, LANES), 1)
    logits = jnp.where(lane < ne, _dot3(h, wr_ref[...]), NEG_BIG)
    m1 = jnp.max(logits, axis=-1, keepdims=True)
    i1 = jnp.min(jnp.where(logits == m1, lane, LANES), axis=-1, keepdims=True)
    sel1 = lane == i1
    rest = jnp.where(sel1, NEG_BIG, logits)
    m2 = jnp.max(rest, axis=-1, keepdims=True)
    i2 = jnp.min(jnp.where(rest == m2, lane, LANES), axis=-1, keepdims=True)
    sel2 = lane == i2
    e2 = jnp.exp(m2 - m1)
    w1 = 1.0 / (1.0 + e2)
    comb = jnp.where(sel1, w1, 0.0) + jnp.where(sel2, e2 * w1, 0.0)
    sel = jnp.where(sel1 | sel2, 1.0, 0.0)
    below = jnp.where(_tri(t, "strict_lower"), 1.0, 0.0).astype(BF16)
    slot = jnp.dot(below, sel.astype(BF16), preferred_element_type=F32)
    comb_ref[...] = comb.T
    pos_ref[...] = jnp.where(sel > 0.0, slot, -1.0).T
    cnt = jnp.sum(sel, axis=0, keepdims=True).astype(jnp.int32)
    cnt_ref[...] = jnp.broadcast_to(cnt, cnt_ref.shape)


def _moe_router(x, nw, sc, sh, w_router, tile):
    b, s, d = x.shape
    ne = w_router.shape[1]
    nt = s // tile
    vec = pl.BlockSpec((None, 1, d), lambda bi, i: (bi, 0, 0))
    return pl.pallas_call(
        functools.partial(_router_kernel, ne=ne),
        grid=(b, nt),
        in_specs=[pl.BlockSpec((None, tile, d), lambda bi, i: (bi, i, 0)),
                  pl.BlockSpec((1, d), lambda bi, i: (0, 0)), vec, vec,
                  pl.BlockSpec((d, LANES), lambda bi, i: (0, 0))],
        out_specs=[pl.BlockSpec((None, tile, d), lambda bi, i: (bi, i, 0)),
                   pl.BlockSpec((None, LANES, tile), lambda bi, i: (bi, 0, i)),
                   pl.BlockSpec((None, LANES, tile), lambda bi, i: (bi, 0, i)),
                   pl.BlockSpec((None, None, 8, LANES), lambda bi, i: (bi, i, 0, 0))],
        out_shape=[jax.ShapeDtypeStruct((b, s, d), BF16),
                   jax.ShapeDtypeStruct((b, LANES, s), F32),
                   jax.ShapeDtypeStruct((b, LANES, s), F32),
                   jax.ShapeDtypeStruct((b, nt, 8, LANES), jnp.int32)],
        compiler_params=_cparams("parallel", "parallel"),
        name="moe_router",
    )(x, nw.reshape(1, d), sc, sh, _pad_cols(w_router, LANES))


def _expert_kernel(cnt_ref, x_ref, hb_ref, pos_ref, comb_ref, g_ref, wg_ref, wu_ref, wd_ref, o_ref,
                   hg_scr, yacc_scr, *, rows, ne, tile, nsub):
    bi, gi, e, f = (pl.program_id(a) for a in range(4))
    nf = pl.num_programs(3)
    ntiles = pl.num_programs(1) * nsub

    def run_tile(sub, carry):
        cnt = cnt_ref[(bi * ntiles + gi * nsub + sub) * ne + e]
        nblk = lax.shift_right_logical(cnt + (rows - 1), int(math.log2(rows)))
        tok = pl.ds(pl.multiple_of(sub * tile, tile), tile)

        @pl.when((e == 0) & (f == 0))
        def _():
            o_ref[tok, :] = x_ref[tok, :]

        def run_rows(row0, m):
            rs = pl.ds(row0, m)

            def one_hot():
                slot = pos_ref[sub, pl.ds(e, 1), :]
                want = (lax.broadcasted_iota(jnp.int32, (m, 1), 0) + row0).astype(F32)
                return slot == want

            @pl.when(f == 0)
            def _():
                p = jnp.where(one_hot(), 1.0, 0.0).astype(BF16)
                hg_scr[sub, rs, :] = jnp.dot(p, hb_ref[tok, :],
                                             preferred_element_type=F32).astype(BF16)

            hg = hg_scr[sub, rs, :]
            gate = jnp.dot(hg, wg_ref[...], preferred_element_type=F32)
            up = jnp.dot(hg, wu_ref[...], preferred_element_type=F32)
            act = (gate * _sigmoid(gate) * up).astype(BF16)
            part = jnp.dot(act, wd_ref[...], preferred_element_type=F32)

            @pl.when(f == 0)
            def _():
                yacc_scr[sub, rs, :] = part

            @pl.when(f > 0)
            def _():
                yacc_scr[sub, rs, :] += part

            @pl.when(f == nf - 1)
            def _():
                hit = one_hot()
                wrow = comb_ref[sub, pl.ds(e, 1), :]
                wgt = jnp.sum(jnp.where(hit, wrow, 0.0), axis=-1, keepdims=True)
                yw = (yacc_scr[sub, rs, :] * wgt * g_ref[...]).astype(BF16)
                p = jnp.where(hit, 1.0, 0.0).astype(BF16)
                o_ref[tok, :] += lax.dot_general(p, yw, (((0,), (0,)), ((), ())),
                                                 preferred_element_type=F32)

        lo = 0
        for m in MOE_SPECIALISED_ROWS:
            @pl.when((cnt > lo) & (cnt <= m))
            def _():
                run_rows(0, m)
            lo = m

        @pl.when(cnt > lo)
        def _():
            def body(j, c):
                run_rows(pl.multiple_of(j * rows, rows), rows)
                return c
            lax.fori_loop(0, nblk, body, 0)

        return carry

    lax.fori_loop(0, nsub, run_tile, 0)


def _moe_experts(x, hb, pos, comb, counts, g, w_gu_bf16, w_down_bf16, layer, tile, nsub, rows, tf=896):
    b, s, d = x.shape
    _, ne, dff, _ = w_down_bf16.shape
    nf = dff // tf
    grp = tile * nsub
    once = pl.Buffered(1)
    grid_spec = pltpu.PrefetchScalarGridSpec(
        num_scalar_prefetch=1,
        grid=(b, s // grp, ne, nf),
        in_specs=[
            pl.BlockSpec((None, grp, d), lambda bi, i, e, f, c: (bi, i, 0), pipeline_mode=once),
            pl.BlockSpec((None, grp, d), lambda bi, i, e, f, c: (bi, i, 0), pipeline_mode=once),
            pl.BlockSpec((None, nsub, 8, tile), lambda bi, i, e, f, c: (bi, i, 0, 0)),
            pl.BlockSpec((None, nsub, 8, tile), lambda bi, i, e, f, c: (bi, i, 0, 0)),
            pl.BlockSpec((None, 1, d), lambda bi, i, e, f, c: (bi, 0, 0)),
            pl.BlockSpec((None, None, d, tf), lambda bi, i, e, f, c: (layer, e, 0, f)),
            pl.BlockSpec((None, None, d, tf), lambda bi, i, e, f, c: (layer, e, 0, f + nf)),
            pl.BlockSpec((None, None, tf, d), lambda bi, i, e, f, c: (layer, e, f, 0)),
        ],
        out_specs=pl.BlockSpec((None, grp, d), lambda bi, i, e, f, c: (bi, i, 0)),
        scratch_shapes=[pltpu.VMEM((nsub, tile, d), BF16), pltpu.VMEM((nsub, tile, d), F32)],
    )
    return pl.pallas_call(
        functools.partial(_expert_kernel, rows=rows, ne=ne, tile=tile, nsub=nsub),
        grid_spec=grid_spec,
        out_shape=jax.ShapeDtypeStruct((b, s, d), F32),
        input_output_aliases={1: 0},
        compiler_params=_cparams("parallel", "parallel", "arbitrary", "arbitrary"),
        name="moe_experts",
    )(counts, x, hb, pos, comb, g, w_gu_bf16, w_gu_bf16, w_down_bf16)


def _moe_layer(x, nw, sc, sh, g, w_router, w_gu_bf16, w_down_bf16, layer):
    b, s, _ = x.shape
    ne = w_router.shape[1]
    tile = min(MOE_TILE, s)
    nsub = min(MOE_GROUP, s // tile)
    hb, comb_t, pos_t, counts = _moe_router(x, nw, sc, sh, w_router, tile)
    counts = counts[:, :, 0, :ne].reshape(-1)
    by_tile = lambda a: jnp.swapaxes(a[:, :8].reshape(b, 8, s // tile, tile), 1, 2)
    return _moe_experts(x, hb, by_tile(pos_t), by_tile(comb_t), counts, g, w_gu_bf16, w_down_bf16,
                        layer, tile, nsub, MOE_ROWS)


def _final_norm_kernel(x_ref, w_ref, o_ref):
    x = x_ref[...]
    ms = jnp.mean(x * x, axis=-1, keepdims=True)
    o_ref[...] = x * lax.rsqrt(ms + RMS_EPS) * w_ref[...]


def _final_norm(x, w, tm=1024):
    b, s, d = x.shape
    return pl.pallas_call(
        _final_norm_kernel,
        grid=(b, s // tm),
        in_specs=[pl.BlockSpec((None, tm, d), lambda bi, i: (bi, i, 0)),
                  pl.BlockSpec((1, d), lambda bi, i: (0, 0))],
        out_specs=pl.BlockSpec((None, tm, d), lambda bi, i: (bi, i, 0)),
        out_shape=jax.ShapeDtypeStruct((b, s, d), F32),
        compiler_params=_cparams("parallel", "parallel"),
        name="final_norm",
    )(x, w.reshape(1, d))


ML_HEADS = 4
SSD_GROUPS, SSD_HEADDIM, SSD_STATE = 4, 64, 128
RW_HEADDIM = 64


def kernel(x, c, ada_w, ada_b, norm_w, final_norm_w, ml_w_in, ml_gate_b, ml_norm_w, ml_w_out, ssd_w_in, ssd_conv_w, ssd_conv_b, ssd_dt_bias, ssd_a_log, ssd_d, ssd_norm_w, ssd_w_out, rw_w_in, rw_mu, rw_w0, rw_w_lora_b, rw_a0, rw_a_lora_b, rw_g_lora_b, rw_k_k, rw_k_a, rw_r_k, rw_ln_w, rw_ln_b, rw_w_out, lru_w_in, lru_conv_w, lru_conv_b, lru_gx_w, lru_gx_b, lru_ga_w, lru_ga_b, lru_lambda, lru_w_out, ffn_w_gu, ffn_w_down, moe_router, moe_w_gu, moe_w_down):
    depth = ada_w.shape[0]
    mod = _ada_modulation(c, ada_w, ada_b)
    moe_gu_b, moe_down_b = moe_w_gu.astype(BF16), moe_w_down.astype(BF16)
    for layer in range(depth):
        sh_t, sc_t, g_t, sh_c, sc_c, g_c = [m[:, None, :] for m in jnp.split(mod[layer], 6, axis=-1)]
        kind, j = layer % 4, layer // 4
        nw_t, nw_c = norm_w[layer, 0], norm_w[layer, 1]
        if kind == 0:
            x = _mlstm_layer(x, nw_t, sc_t, sh_t, g_t, ml_w_in[j], ml_gate_b[j], ml_norm_w[j],
                             ml_w_out[j], ML_HEADS)
        elif kind == 1:
            x = _ssd_layer(x, nw_t, sc_t, sh_t, g_t, ssd_w_in[j], ssd_conv_w[j], ssd_conv_b[j],
                           ssd_dt_bias[j], ssd_a_log[j], ssd_d[j], ssd_norm_w[j], ssd_w_out[j],
                           SSD_GROUPS, SSD_HEADDIM, SSD_STATE)
        elif kind == 2:
            x = _rwkv_layer(x, nw_t, sc_t, sh_t, g_t, rw_w_in[j], rw_mu[j], rw_w0[j], rw_w_lora_b[j],
                            rw_a0[j], rw_a_lora_b[j], rw_g_lora_b[j], rw_k_k[j], rw_k_a[j], rw_r_k[j],
                            rw_ln_w[j], rw_ln_b[j], rw_w_out[j], RW_HEADDIM)
        else:
            x = _rglru_layer(x, nw_t, sc_t, sh_t, g_t, lru_w_in[j], lru_conv_w[j], lru_conv_b[j],
                             lru_gx_w[j], lru_gx_b[j], lru_ga_w[j], lru_ga_b[j], lru_lambda[j],
                             lru_w_out[j])
        if layer % 2 == 0:
            x = _ffn_dense(x, nw_c, sc_c, sh_c, g_c, ffn_w_gu[layer // 2].astype(BF16),
                           ffn_w_down[layer // 2].astype(BF16))
        else:
            x = _moe_layer(x, nw_c, sc_c, sh_c, g_c, moe_router[layer // 2], moe_gu_b, moe_down_b,
                           layer // 2)
    return _final_norm(x, final_norm_w)
```

```python
import functools
import math

import jax
import jax.numpy as jnp
from jax import lax
from jax.experimental import pallas as pl
from jax.experimental.pallas import tpu as pltpu

F32 = jnp.float32
BF16 = jnp.bfloat16

RMS_EPS = 1e-6
LANES = 128
VMEM_LIMIT_BYTES = 56 * 1024 * 1024


def _cparams(*sem):
    return pltpu.CompilerParams(dimension_semantics=sem, vmem_limit_bytes=VMEM_LIMIT_BYTES)


def _sigmoid(x):
    return 1.0 / (1.0 + jnp.exp(-x))


def _softplus(x):
    return jnp.maximum(x, 0.0) + jnp.log(1.0 + jnp.exp(-jnp.abs(x)))


def _bdot(a, b):
    return jnp.dot(a.astype(BF16), b.astype(BF16), preferred_element_type=F32)


def _split_bf16(a):
    hi = a.astype(BF16)
    lo = (a - hi.astype(F32)).astype(BF16)
    return hi, lo


def _dot3(a, b):
    a_hi, a_lo = _split_bf16(a)
    b_hi, b_lo = _split_bf16(b)
    d = functools.partial(jnp.dot, preferred_element_type=F32)
    return d(a_hi, b_hi) + d(a_lo, b_hi) + d(a_hi, b_lo)


def _norm_mod(x, nw, sc, sh):
    ms = jnp.mean(x * x, axis=-1, keepdims=True)
    y = x * lax.rsqrt(ms + RMS_EPS) * nw
    return y * (1.0 + sc) + sh


def _pad_cols(w, n):
    return jnp.pad(w, ((0, 0), (0, n - w.shape[1])))


def _ada_kernel(c_ref, w_ref, b_ref, o_ref):
    c = c_ref[...]
    cond = c * _sigmoid(c)
    o_ref[...] = _dot3(cond, w_ref[...]) + b_ref[...]


def _ada_modulation(c, ada_w, ada_b):
    depth, d, n = ada_w.shape
    b = c.shape[0]
    rows = 8
    cp = jnp.pad(c, ((0, rows - b), (0, 0)))
    tn = 1024
    out = pl.pallas_call(
        _ada_kernel,
        grid=(depth, n // tn),
        in_specs=[
            pl.BlockSpec((rows, d), lambda l, j: (0, 0)),
            pl.BlockSpec((None, d, tn), lambda l, j: (l, 0, j)),
            pl.BlockSpec((None, 1, tn), lambda l, j: (l, 0, j)),
        ],
        out_specs=pl.BlockSpec((None, rows, tn), lambda l, j: (l, 0, j)),
        out_shape=jax.ShapeDtypeStruct((depth, rows, n), F32),
        compiler_params=_cparams("parallel", "parallel"),
        name="ada_modulation",
    )(cp, ada_w, ada_b.reshape(depth, 1, n))
    return out[:, :b]


def _nmm_kernel(x_ref, nw_ref, sc_ref, sh_ref, w_ref, *o_refs, segs):
    hb = _norm_mod(x_ref[...], nw_ref[...], sc_ref[...], sh_ref[...]).astype(BF16)
    off = 0
    for o_ref, n in zip(o_refs, segs):
        o_ref[...] = jnp.dot(hb, w_ref[:, off:off + n],
                             preferred_element_type=F32).astype(o_ref.dtype)
        off += n


def _norm_mod_matmul(x, nw, sc, sh, w_bf16, segs, dtypes, tm=512):
    b, s, d = x.shape
    ntot = w_bf16.shape[1]
    assert sum(segs) == ntot and s % tm == 0
    vec = pl.BlockSpec((None, 1, d), lambda bi, i: (bi, 0, 0))
    return pl.pallas_call(
        functools.partial(_nmm_kernel, segs=tuple(segs)),
        grid=(b, s // tm),
        in_specs=[
            pl.BlockSpec((None, tm, d), lambda bi, i: (bi, i, 0)),
            pl.BlockSpec((1, d), lambda bi, i: (0, 0)),
            vec, vec,
            pl.BlockSpec((d, ntot), lambda bi, i: (0, 0)),
        ],
        out_specs=[pl.BlockSpec((None, tm, n), lambda bi, i: (bi, i, 0)) for n in segs],
        out_shape=[jax.ShapeDtypeStruct((b, s, n), dt) for n, dt in zip(segs, dtypes)],
        compiler_params=_cparams("parallel", "parallel"),
        name="norm_mod_matmul",
    )(x, nw.reshape(1, d), sc, sh, w_bf16)


def _outproj_kernel(x_ref, y_ref, g_ref, w_ref, o_ref):
    o_ref[...] = x_ref[...] + g_ref[...] * jnp.dot(
        y_ref[...], w_ref[...], preferred_element_type=F32)


def _outproj_residual(x, y, g, w_bf16, tm=512):
    b, s, d = x.shape
    k = y.shape[-1]
    return pl.pallas_call(
        _outproj_kernel,
        grid=(b, s // tm),
        in_specs=[
            pl.BlockSpec((None, tm, d), lambda bi, i: (bi, i, 0)),
            pl.BlockSpec((None, tm, k), lambda bi, i: (bi, i, 0)),
            pl.BlockSpec((None, 1, d), lambda bi, i: (bi, 0, 0)),
            pl.BlockSpec((k, d), lambda bi, i: (0, 0)),
        ],
        out_specs=pl.BlockSpec((None, tm, d), lambda bi, i: (bi, i, 0)),
        out_shape=jax.ShapeDtypeStruct((b, s, d), F32),
        input_output_aliases={0: 0},
        compiler_params=_cparams("parallel", "parallel"),
        name="outproj_residual",
    )(x, y, g, w_bf16)


def _ffn_kernel(x_ref, nw_ref, sc_ref, sh_ref, g_ref, wg_ref, wu_ref, wd_ref, o_ref,
                h_scr, acc_scr):
    f = pl.program_id(2)

    @pl.when(f == 0)
    def _():
        h_scr[...] = _norm_mod(x_ref[...], nw_ref[...], sc_ref[...], sh_ref[...]).astype(BF16)
        acc_scr[...] = jnp.zeros_like(acc_scr)

    hb = h_scr[...]
    gate = jnp.dot(hb, wg_ref[...], preferred_element_type=F32)
    up = jnp.dot(hb, wu_ref[...], preferred_element_type=F32)
    act = (gate * _sigmoid(gate) * up).astype(BF16)
    acc_scr[...] += jnp.dot(act, wd_ref[...], preferred_element_type=F32)

    @pl.when(f == pl.num_programs(2) - 1)
    def _():
        o_ref[...] = x_ref[...] + g_ref[...] * acc_scr[...]


def _ffn_dense(x, nw, sc, sh, g, w_gu_bf16, w_down_bf16, tm=1024, tf=512):
    b, s, d = x.shape
    dff = w_down_bf16.shape[0]
    nf = dff // tf
    vec = pl.BlockSpec((None, 1, d), lambda bi, i, f: (bi, 0, 0))
    return pl.pallas_call(
        _ffn_kernel,
        grid=(b, s // tm, nf),
        in_specs=[
            pl.BlockSpec((None, tm, d), lambda bi, i, f: (bi, i, 0)),
            pl.BlockSpec((1, d), lambda bi, i, f: (0, 0)),
            vec, vec, vec,
            pl.BlockSpec((d, tf), lambda bi, i, f: (0, f)),
            pl.BlockSpec((d, tf), lambda bi, i, f: (0, f + nf)),
            pl.BlockSpec((tf, d), lambda bi, i, f: (f, 0)),
        ],
        out_specs=pl.BlockSpec((None, tm, d), lambda bi, i, f: (bi, i, 0)),
        out_shape=jax.ShapeDtypeStruct((b, s, d), F32),
        scratch_shapes=[pltpu.VMEM((tm, d), BF16), pltpu.VMEM((tm, d), F32)],
        input_output_aliases={0: 0},
        compiler_params=_cparams("parallel", "parallel", "arbitrary"),
        name="ffn_dense",
    )(x, nw.reshape(1, d), sc, sh, g, w_gu_bf16, w_gu_bf16, w_down_bf16)


CONV_W = 4
TAIL = 8


def _causal_conv(pad_ref, x, w_ref, b_ref, first):
    l = x.shape[0]

    @pl.when(first)
    def _():
        pad_ref[0:TAIL, :] = jnp.zeros((TAIL, x.shape[1]), F32)

    pad_ref[TAIL:2 * TAIL, :] = x[0:TAIL, :]
    y = b_ref[...] + w_ref[CONV_W - 1:CONV_W, :] * x
    head = y[0:TAIL, :]
    for k in range(CONV_W - 1):
        shift = CONV_W - 1 - k
        y = y + w_ref[k:k + 1, :] * pltpu.roll(x, shift, axis=0)
        head = head + w_ref[k:k + 1, :] * pad_ref[TAIL - shift:2 * TAIL - shift, :]
    pad_ref[0:TAIL, :] = x[l - TAIL:, :]
    return jnp.concatenate([head, y[TAIL:, :]], axis=0)


LRU_C = 8.0


def _gelu_tanh(x):
    return 0.5 * x * (1.0 + jnp.tanh(math.sqrt(2.0 / math.pi) * (x + 0.044715 * (x * x * x))))


def _lru_kernel(xbr_ref, ybr_ref, cw_ref, cb_ref, gxw_ref, gxb_ref, gaw_ref, gab_ref, lam_ref,
                o_ref, pad_scr, h_scr):
    i = pl.program_id(1)
    l, w = xbr_ref.shape
    nblk, blk = gxw_ref.shape[0], gxw_ref.shape[1]

    @pl.when(i == 0)
    def _():
        h_scr[...] = jnp.zeros_like(h_scr)

    xb = _causal_conv(pad_scr, xbr_ref[...], cw_ref, cb_ref, i == 0)
    xbb = xb.astype(BF16)
    gx = jnp.concatenate([jnp.dot(xbb[:, n * blk:(n + 1) * blk], gxw_ref[n],
                                  preferred_element_type=F32) for n in range(nblk)], axis=-1)
    ga = jnp.concatenate([jnp.dot(xbb[:, n * blk:(n + 1) * blk], gaw_ref[n],
                                  preferred_element_type=F32) for n in range(nblk)], axis=-1)
    gate_x = _sigmoid(gx + gxb_ref[...])
    gate_a = _sigmoid(ga + gab_ref[...])
    log_a = LRU_C * gate_a * (-_softplus(-lam_ref[...]))
    a = jnp.exp(log_a)
    u = jnp.sqrt(1.0 - jnp.exp(2.0 * log_a)) * gate_x * xb

    row = lax.broadcasted_iota(jnp.int32, (l, 1), 0)
    d = 1
    while d < l:
        if d < TAIL:
            a_sh = pltpu.roll(a, d, axis=0)
            u_sh = pltpu.roll(u, d, axis=0)
            valid = row >= d
            u = jnp.where(valid, a * u_sh + u, u)
            a = jnp.where(valid, a * a_sh, a)
        else:
            u = jnp.concatenate([u[:d], a[d:] * u[:l - d] + u[d:]], axis=0)
            a = jnp.concatenate([a[:d], a[d:] * a[:l - d]], axis=0)
        d *= 2
    hs = a * h_scr[0:1, :] + u
    h_scr[...] = jnp.broadcast_to(hs[l - 1:l, :], h_scr.shape)
    o_ref[...] = (_gelu_tanh(ybr_ref[...].astype(F32)) * hs).astype(o_ref.dtype)


def _rglru_core(x_br, y_br, conv_w, conv_b, gx_w, gx_b, ga_w, ga_b, lam, chunk=256):
    b, s, w = x_br.shape
    nblk, blk, _ = gx_w.shape
    row = lambda a: a.reshape(1, w).astype(F32)
    full2 = lambda shape: pl.BlockSpec(shape, lambda bi, i: (0, 0))
    full3 = lambda shape: pl.BlockSpec(shape, lambda bi, i: (0, 0, 0))
    return pl.pallas_call(
        _lru_kernel,
        grid=(b, s // chunk),
        in_specs=[
            pl.BlockSpec((None, chunk, w), lambda bi, i: (bi, i, 0)),
            pl.BlockSpec((None, chunk, w), lambda bi, i: (bi, i, 0)),
            full2((CONV_W, w)), full2((1, w)),
            full3((nblk, blk, blk)), full2((1, w)),
            full3((nblk, blk, blk)), full2((1, w)),
            full2((1, w)),
        ],
        out_specs=pl.BlockSpec((None, chunk, w), lambda bi, i: (bi, i, 0)),
        out_shape=jax.ShapeDtypeStruct((b, s, w), BF16),
        scratch_shapes=[pltpu.VMEM((2 * TAIL, w), F32), pltpu.VMEM((8, w), F32)],
        compiler_params=_cparams("parallel", "arbitrary"),
        name="rglru_core",
    )(x_br, y_br, conv_w, row(conv_b), gx_w.astype(BF16), row(gx_b), ga_w.astype(BF16), row(ga_b),
      row(lam))


def _rglru_layer(x, nw, sc, sh, g, w_in, conv_w, conv_b, gx_w, gx_b, ga_w, ga_b, lam, w_out):
    w = w_in.shape[1] // 2
    y_br, x_br = _norm_mod_matmul(x, nw, sc, sh, w_in.astype(BF16), [w, w], [BF16, F32])
    out = _rglru_core(x_br, y_br, conv_w, conv_b, gx_w, gx_b, ga_w, ga_b, lam)
    return _outproj_residual(x, out, g, w_out.astype(BF16))


NEG_BIG = -1e30


def _split3(a):
    hi = a.astype(BF16)
    r = a - hi.astype(F32)
    mid = r.astype(BF16)
    lo = (r - mid.astype(F32)).astype(BF16)
    return hi, mid, lo


def _tri(l, kind):
    r = lax.broadcasted_iota(jnp.int32, (l, l), 0)
    c = lax.broadcasted_iota(jnp.int32, (l, l), 1)
    m = {"lower": r >= c, "strict_lower": r > c, "upper": r <= c}[kind]
    return m


def _cumsum_rows(a):
    t = jnp.where(_tri(a.shape[0], "lower"), 1.0, 0.0).astype(BF16)
    return sum(jnp.dot(t, p, preferred_element_type=F32) for p in _split3(a))


def _cumsum_lanes(a):
    t = jnp.where(_tri(a.shape[1], "upper"), 1.0, 0.0).astype(BF16)
    return sum(jnp.dot(p, t, preferred_element_type=F32) for p in _split3(a))


def _mlstm_kernel(q_ref, k_ref, v_ref, o_ref, gc_ref, gr_ref, gbr_ref, gbc_ref, nw_ref, y_ref,
                  c_scr, n_scr, m_scr, *, heads, dk, dv):
    i = pl.program_id(1)
    l = q_ref.shape[0]

    @pl.when(i == 0)
    def _():
        c_scr[...] = jnp.zeros_like(c_scr)
        n_scr[...] = jnp.zeros_like(n_scr)
        m_scr[...] = jnp.zeros_like(m_scr)

    gc = gc_ref[...] + gbr_ref[...]
    gr = gr_ref[...] + gbc_ref[...]
    li_col = gc[:, 0:heads]
    li_row = gr[0:heads, :]
    g_col = _cumsum_rows(-_softplus(-gc))[:, heads:2 * heads]
    g_row = _cumsum_lanes(-_softplus(-gr))[heads:2 * heads, :]
    causal = _tri(l, "lower")
    scale = dk ** -0.5

    st = []
    for h in range(heads):
        a_col = g_col[:, h:h + 1]
        b_row = li_row[h:h + 1, :] - g_row[h:h + 1, :]
        m_prev = m_scr[h, 0:1, 0:1]
        logd = jnp.where(causal, a_col + b_row, NEG_BIG)
        log_inter = a_col + m_prev
        m_row = jnp.maximum(jnp.max(logd, axis=-1, keepdims=True), log_inter)
        qs = (q_ref[:, h * dk:(h + 1) * dk].astype(F32) * scale).astype(BF16)
        k = k_ref[:, h * dk:(h + 1) * dk]
        c_mat = c_scr[h]
        st.append(dict(
            a_col=a_col, b_col=li_col[:, h:h + 1] - a_col, m_prev=m_prev, m_row=m_row, qs=qs, k=k,
            c_mat=c_mat, e_inter=jnp.exp(log_inter - m_row),
            s=lax.dot_general(qs, k, (((1,), (1,)), ((), ())), preferred_element_type=F32)
            * jnp.exp(logd - m_row),
            qc=jnp.dot(qs, c_mat.astype(BF16), preferred_element_type=F32)))

    for h, t in enumerate(st):
        v = v_ref[:, h * dv:(h + 1) * dv]
        n_vec = n_scr[h, 0:1, :]
        num = jnp.dot(t["s"].astype(BF16), v, preferred_element_type=F32) + t["e_inter"] * t["qc"]
        den = (jnp.sum(t["s"], axis=-1, keepdims=True)
               + t["e_inter"] * jnp.sum(t["qs"].astype(F32) * n_vec, axis=-1, keepdims=True))
        denom = jnp.maximum(jnp.abs(den), jnp.exp(-t["m_row"]))
        hh = num / denom
        hh = hh * lax.rsqrt(jnp.mean(hh * hh, axis=-1, keepdims=True) + RMS_EPS)
        gate = _sigmoid(o_ref[:, h * dv:(h + 1) * dv].astype(F32))
        y_ref[:, h * dv:(h + 1) * dv] = (hh * nw_ref[:, h * dv:(h + 1) * dv] * gate).astype(y_ref.dtype)

    for h, t in enumerate(st):
        v = v_ref[:, h * dv:(h + 1) * dv]
        n_vec = n_scr[h, 0:1, :]
        g_last = t["a_col"][l - 1:l, :]
        log_w = g_last + t["b_col"]
        m_new = jnp.maximum(g_last + t["m_prev"], jnp.max(log_w, axis=0, keepdims=True))
        kw = t["k"].astype(F32) * jnp.exp(log_w - m_new)
        decay = jnp.exp(g_last + t["m_prev"] - m_new)
        c_scr[h] = decay * t["c_mat"] + lax.dot_general(
            kw.astype(BF16), v, (((0,), (0,)), ((), ())), preferred_element_type=F32)
        n_scr[h, 0:1, :] = decay * n_vec + jnp.sum(kw, axis=0, keepdims=True)
        m_scr[h] = jnp.broadcast_to(m_new, m_scr.shape[1:])


def _mlstm_core(q, k, v, o, gates, gate_b, norm_w, heads, chunk=128):
    b, s, hk = q.shape
    hv = v.shape[-1]
    dk, dv = hk // heads, hv // heads
    g2 = 2 * heads
    gates_t = jnp.swapaxes(gates, 1, 2)
    gb = gate_b.reshape(1, g2).astype(F32)
    tok = lambda n: pl.BlockSpec((None, chunk, n), lambda bi, i: (bi, i, 0))
    full = lambda shape: pl.BlockSpec(shape, lambda bi, i: (0, 0))
    return pl.pallas_call(
        functools.partial(_mlstm_kernel, heads=heads, dk=dk, dv=dv),
        grid=(b, s // chunk),
        in_specs=[tok(hk), tok(hk), tok(hv), tok(hv), tok(g2),
                  pl.BlockSpec((None, g2, chunk), lambda bi, i: (bi, 0, i)),
                  full((1, g2)), full((g2, 1)), full((1, hv))],
        out_specs=tok(hv),
        out_shape=jax.ShapeDtypeStruct((b, s, hv), BF16),
        scratch_shapes=[pltpu.VMEM((heads, dk, dv), F32), pltpu.VMEM((heads, 8, dk), F32),
                        pltpu.VMEM((heads, 8, LANES), F32)],
        compiler_params=_cparams("parallel", "arbitrary"),
        name="mlstm_core",
    )(q, k, v, o, gates, gates_t, gb, gb.reshape(g2, 1), norm_w.reshape(1, hv).astype(F32))


def _mlstm_layer(x, nw, sc, sh, g, w_in, gate_b, norm_w, w_out, heads):
    hv = w_out.shape[0]
    hk = (w_in.shape[1] - 2 * hv - 2 * heads) // 2
    w_pad = _pad_cols(w_in, 2 * hk + 2 * hv + LANES).astype(BF16)
    q, k, v, o, gates = _norm_mod_matmul(x, nw, sc, sh, w_pad, [hk, hk, hv, hv, LANES],
                                         [BF16, BF16, BF16, BF16, F32])
    y = _mlstm_core(q, k, v, o, gates[..., :2 * heads], gate_b, norm_w, heads)
    return _outproj_residual(x, y, g, w_out.astype(BF16))


def _pair_select(lo_half, a, b):
    return jnp.where(lo_half, a, b)


def _ssd_kernel(z_ref, xbc_ref, dtc_ref, dtr_ref, cw_ref, cb_ref, dbr_ref, dbc_ref, alr_ref, alc_ref,
                dsk_ref, nw_ref, y_ref, pad_scr, st_scr, *, groups, hpg, hd, ns):
    i = pl.program_id(1)
    l = z_ref.shape[0]
    inner = groups * hpg * hd
    gw = hpg * hd
    pair = 2 * hd

    @pl.when(i == 0)
    def _():
        st_scr[...] = jnp.zeros_like(st_scr)

    conv = _causal_conv(pad_scr, xbc_ref[...].astype(F32), cw_ref, cb_ref, i == 0)
    conv = conv * _sigmoid(conv)
    xs = conv[:, 0:inner]
    xs_b = xs.astype(BF16)
    bm = conv[:, inner:inner + groups * ns].astype(BF16)
    cm = conv[:, inner + groups * ns:inner + 2 * groups * ns].astype(BF16)

    dt_col = _softplus(dtc_ref[...] + dbr_ref[...])
    dt_row = _softplus(dtr_ref[...] + dbc_ref[...])
    acum_col = _cumsum_rows(dt_col * (-jnp.exp(alr_ref[...])))
    acum_row = _cumsum_lanes(dt_row * (-jnp.exp(alc_ref[...])))
    a_last = acum_col[l - 1:l, :]
    ea_col = jnp.exp(acum_col)
    ws_col = jnp.exp(a_last - acum_col) * dt_col
    ea_last = jnp.exp(a_last)

    causal = _tri(l, "lower")
    lo_half = lax.broadcasted_iota(jnp.int32, (1, pair), 1) < hd

    for g in range(groups):
        bg = bm[:, g * ns:(g + 1) * ns]
        cg = cm[:, g * ns:(g + 1) * ns]
        cb = lax.dot_general(cg, bg, (((1,), (1,)), ((), ())), preferred_element_type=F32)
        st = st_scr[g]
        y_inter = jnp.dot(cg, st.astype(BF16), preferred_element_type=F32)
        xw_parts, y_parts, dec_parts = [], [], []
        for p in range(hpg // 2):
            h0 = g * hpg + 2 * p
            c0 = g * gw + p * pair
            xp = xs_b[:, c0:c0 + pair]
            ys = []
            for h in (h0, h0 + 1):
                dec = jnp.exp(jnp.where(causal, acum_col[:, h:h + 1] - acum_row[h:h + 1, :], NEG_BIG))
                wts = (cb * dec * dt_row[h:h + 1, :]).astype(BF16)
                ys.append(jnp.dot(wts, xp, preferred_element_type=F32))
            sel = lambda a: _pair_select(lo_half, a[:, h0:h0 + 1], a[:, h0 + 1:h0 + 2])
            y_parts.append(_pair_select(lo_half, ys[0], ys[1])
                           + sel(ea_col) * y_inter[:, p * pair:(p + 1) * pair])
            xw_parts.append((xs[:, c0:c0 + pair] * sel(ws_col)).astype(BF16))
            dec_parts.append(sel(ea_last))
        xw = jnp.concatenate(xw_parts, axis=-1)
        st_scr[g] = jnp.concatenate(dec_parts, axis=-1) * st + lax.dot_general(
            bg, xw, (((0,), (0,)), ((), ())), preferred_element_type=F32)
        yg = jnp.concatenate(y_parts, axis=-1)
        sl = slice(g * gw, (g + 1) * gw)
        yg = yg + dsk_ref[:, sl] * xs[:, sl]
        zg = z_ref[:, sl].astype(F32)
        yg = yg * (zg * _sigmoid(zg))
        yg = yg * lax.rsqrt(jnp.mean(yg * yg, axis=-1, keepdims=True) + RMS_EPS)
        y_ref[:, sl] = (yg * nw_ref[:, sl]).astype(y_ref.dtype)


def _ssd_core(z, xbc, dt, conv_w, conv_b, dt_bias, a_log, d_skip, norm_w, groups, hd, ns, chunk=128):
    b, s, inner = z.shape
    heads = dt.shape[-1]
    hpg = heads // groups
    cdim = xbc.shape[-1]
    dt_t = jnp.swapaxes(dt, 1, 2)
    rowv = lambda a: a.reshape(1, -1).astype(F32)
    colv = lambda a: a.reshape(-1, 1).astype(F32)
    tok = lambda n: pl.BlockSpec((None, chunk, n), lambda bi, i: (bi, i, 0))
    full = lambda shape: pl.BlockSpec(shape, lambda bi, i: (0, 0))
    return pl.pallas_call(
        functools.partial(_ssd_kernel, groups=groups, hpg=hpg, hd=hd, ns=ns),
        grid=(b, s // chunk),
        in_specs=[tok(inner), tok(cdim), tok(heads),
                  pl.BlockSpec((None, heads, chunk), lambda bi, i: (bi, 0, i)),
                  full((CONV_W, cdim)), full((1, cdim)),
                  full((1, heads)), full((heads, 1)), full((1, heads)), full((heads, 1)),
                  full((1, inner)), full((1, inner))],
        out_specs=tok(inner),
        out_shape=jax.ShapeDtypeStruct((b, s, inner), BF16),
        scratch_shapes=[pltpu.VMEM((2 * TAIL, cdim), F32),
                        pltpu.VMEM((groups, ns, hpg * hd), F32)],
        compiler_params=_cparams("parallel", "arbitrary"),
        name="ssd_core",
    )(z, xbc, dt, dt_t, conv_w, rowv(conv_b), rowv(dt_bias), colv(dt_bias), rowv(a_log), colv(a_log),
      rowv(jnp.repeat(d_skip, hd)), rowv(norm_w))


def _ssd_layer(x, nw, sc, sh, g, w_in, conv_w, conv_b, dt_bias, a_log, d_skip, norm_w, w_out,
               groups, hd, ns):
    inner = w_out.shape[0]
    heads = inner // hd
    cdim = inner + 2 * groups * ns
    w_pad = _pad_cols(w_in, inner + cdim + LANES).astype(BF16)
    z, xbc, dt = _norm_mod_matmul(x, nw, sc, sh, w_pad, [inner, cdim, LANES], [BF16, BF16, F32])
    y = _ssd_core(z, xbc, dt[..., :heads], conv_w, conv_b, dt_bias, a_log, d_skip, norm_w,
                  groups, hd, ns)
    return _outproj_residual(x, y, g, w_out.astype(BF16))


def _rwkv_inproj_kernel(x_ref, nw_ref, sc_ref, sh_ref, mu_ref, wr_ref, wk_ref, wv_ref,
                        wdw_ref, wda_ref, wdg_ref, w0_ref, wlb_ref, a0_ref, alb_ref, glb_ref,
                        r_ref, k_ref, v_ref, lw_ref, a_ref, g_ref, last_scr):
    i = pl.program_id(1)
    tm = x_ref.shape[0]

    @pl.when(i == 0)
    def _():
        last_scr[...] = jnp.zeros_like(last_scr)

    h = _norm_mod(x_ref[...], nw_ref[...], sc_ref[...], sh_ref[...])
    row = lax.broadcasted_iota(jnp.int32, (tm, 1), 0)
    h_prev = jnp.where(row == 0, last_scr[0:1, :], pltpu.roll(h, 1, axis=0))
    last_scr[...] = jnp.broadcast_to(h[tm - 1:tm, :], last_scr.shape)
    xx = h_prev - h

    def proj(b, w_ref):
        xb = (h + xx * mu_ref[b:b + 1, :]).astype(BF16)
        return jnp.dot(xb, w_ref[...], preferred_element_type=F32)

    r_ref[...] = proj(0, wr_ref).astype(r_ref.dtype)
    k_ref[...] = proj(1, wk_ref).astype(k_ref.dtype)
    v_ref[...] = proj(2, wv_ref).astype(v_ref.dtype)
    dw = proj(3, wdw_ref)
    da = proj(4, wda_ref)
    dg = proj(5, wdg_ref)
    z = w0_ref[...] + _bdot(jnp.tanh(dw), wlb_ref[...])
    w_log = -_softplus(-z) - 0.5
    lw_ref[...] = -jnp.exp(w_log)
    a_ref[...] = _sigmoid(a0_ref[...] + _bdot(da, alb_ref[...])).astype(a_ref.dtype)
    g_ref[...] = _bdot(_sigmoid(dg), glb_ref[...]).astype(g_ref.dtype)


def _pad_rows(w, n):
    return jnp.pad(w, ((0, n - w.shape[0]), (0, 0)))


def _round_up(n, m):
    return -(-n // m) * m


def _rwkv_inproj(x, nw, sc, sh, w_in, mu, w0, w_lora_b, a0, a_lora_b, g_lora_b, tm=256):
    b, s, d = x.shape
    nl = [w_lora_b.shape[0], a_lora_b.shape[0], g_lora_b.shape[0]]
    nlp = [_round_up(n, LANES) for n in nl]
    offs = [0, d, 2 * d, 3 * d, 3 * d + nl[0], 3 * d + nl[0] + nl[1]]
    wr, wk, wv = (w_in[:, offs[j]:offs[j] + d].astype(BF16) for j in range(3))
    wl = [_pad_cols(w_in[:, offs[3 + j]:offs[3 + j] + nl[j]], nlp[j]).astype(BF16) for j in range(3)]
    lb = [_pad_rows(m, n).astype(BF16) for m, n in zip((w_lora_b, a_lora_b, g_lora_b), nlp)]
    rowv = lambda a: a.reshape(1, d).astype(F32)
    tok = pl.BlockSpec((None, tm, d), lambda bi, i: (bi, i, 0))
    vec = pl.BlockSpec((None, 1, d), lambda bi, i: (bi, 0, 0))
    full = lambda a: pl.BlockSpec(a.shape, lambda bi, i: (0, 0))
    ins = [x, rowv(nw), sc, sh, mu.astype(F32), wr, wk, wv, wl[0], wl[1], wl[2],
           rowv(w0), lb[0], rowv(a0), lb[1], lb[2]]
    specs = [tok, full(ins[1]), vec, vec] + [full(a) for a in ins[4:]]
    dts = [BF16, BF16, BF16, F32, BF16, BF16]
    return pl.pallas_call(
        _rwkv_inproj_kernel,
        grid=(b, s // tm),
        in_specs=specs,
        out_specs=[tok] * 6,
        out_shape=[jax.ShapeDtypeStruct((b, s, d), dt) for dt in dts],
        scratch_shapes=[pltpu.VMEM((8, d), F32)],
        compiler_params=_cparams("parallel", "arbitrary"),
        name="rwkv_inproj",
    )(*ins)


RW_GN_EPS = 64e-5


def _rwkv_kernel(r_ref, k_ref, v_ref, lw_ref, a_ref, g_ref, kk_ref, ka_ref, rk_ref, lnw_ref, lnb_ref,
                 y_ref, s_scr, *, hd):
    i = pl.program_id(1)
    l, d = r_ref.shape
    pair = 2 * hd
    npairs = d // pair

    @pl.when(i == 0)
    def _():
        s_scr[...] = jnp.zeros_like(s_scr)

    lw = lw_ref[...]
    cum = _cumsum_rows(lw)
    lo1 = lax.broadcasted_iota(jnp.int32, (1, pair), 1) < hd
    r2 = lax.broadcasted_iota(jnp.int32, (2 * l, 1), 0) < l
    c2 = lax.broadcasted_iota(jnp.int32, (1, pair), 1) < hd
    stack_mask = r2 == c2
    rr = lax.broadcasted_iota(jnp.int32, (2 * l, 2 * l), 0)
    cc = lax.broadcasted_iota(jnp.int32, (2 * l, 2 * l), 1)
    same = (rr < l) == (cc < l)
    strict = same & (rr > cc)
    incl = same & (rr >= cc)
    kr = lax.broadcasted_iota(jnp.int32, (pair, pair), 0) < hd
    kc = lax.broadcasted_iota(jnp.int32, (pair, pair), 1) < hd
    bd = kr == kc

    def half_sum(t):
        s_lo = jnp.sum(jnp.where(lo1, t, 0.0), axis=-1, keepdims=True)
        s_hi = jnp.sum(jnp.where(lo1, 0.0, t), axis=-1, keepdims=True)
        return jnp.where(lo1, s_lo, s_hi)

    def stack(t, masked):
        t2 = jnp.concatenate([t, t], axis=0)
        return jnp.where(stack_mask, t2, 0.0) if masked else t2

    nt = (((1,), (1,)), ((), ()))
    tn = (((0,), (0,)), ((), ()))
    pairs = range(npairs)
    sls = [slice(p * pair, (p + 1) * pair) for p in pairs]
    st = []
    for p in pairs:
        sl = sls[p]
        r = r_ref[:, sl].astype(F32)
        k = k_ref[:, sl].astype(F32)
        v = v_ref[:, sl].astype(F32)
        a = a_ref[:, sl].astype(F32)
        cm = cum[:, sl]
        gam = jnp.exp(cm)
        gam_prev = jnp.exp(cm - lw[:, sl])
        inv_gam = jnp.exp(-cm)
        kk = k * kk_ref[:, sl]
        kk = kk / jnp.maximum(jnp.sqrt(half_sum(kk * kk)), 1e-12)
        k2 = k * (1.0 + (a - 1.0) * ka_ref[:, sl])
        at = stack(-kk * gam_prev, True).astype(BF16)
        rt = stack(r * gam, True).astype(BF16)
        bt = (kk * a * inv_gam).astype(BF16)
        kt = (k2 * inv_gam).astype(BF16)
        v_st = stack(v, False)
        s0 = s_scr[p]
        lhs = jnp.concatenate([at, rt], axis=0)
        bk = jnp.concatenate([bt, bt, kt, kt], axis=0)
        big = lax.dot_general(lhs, bk, nt, preferred_element_type=F32)
        ls = lax.dot_general(lhs, s0.astype(BF16), nt, preferred_element_type=F32)
        st.append(dict(
            v_st=v_st, v_b=v_st.astype(BF16), bk=bk, s0=s0, gam_last=gam[l - 1:l, :],
            bonus=half_sum(r * k2 * rk_ref[:, sl]) * v,
            m=jnp.where(strict, big[0:2 * l, 0:2 * l], 0.0).astype(BF16),
            a_ak=jnp.where(strict, big[0:2 * l, 2 * l:4 * l], 0.0).astype(BF16),
            a_r=jnp.concatenate([jnp.where(incl, big[2 * l:4 * l, 0:2 * l], 0.0),
                                 jnp.where(incl, big[2 * l:4 * l, 2 * l:4 * l], 0.0)],
                                axis=1).astype(BF16),
            ls_a=ls[0:2 * l], ls_r=ls[2 * l:4 * l]))

    for q in st:
        q["x"] = q["ls_a"] + jnp.dot(q["a_ak"], q["v_b"], preferred_element_type=F32)
    span = 1
    while span < l:
        for q in st:
            q["x"] = q["x"] + jnp.dot(q["m"], q["x"].astype(BF16), preferred_element_type=F32)
        span *= 2
        if span < l:
            for q in st:
                q["m"] = jnp.dot(q["m"], q["m"], preferred_element_type=F32).astype(BF16)

    for p in pairs:
        q, sl = st[p], sls[p]
        x_sol = q["x"]
        y_st = q["ls_r"] + jnp.dot(q["a_r"], jnp.concatenate([x_sol.astype(BF16), q["v_b"]], axis=0),
                                   preferred_element_type=F32)
        uv = jnp.concatenate([jnp.where(stack_mask, x_sol, 0.0),
                              jnp.where(stack_mask, q["v_st"], 0.0)], axis=0).astype(BF16)
        upd = lax.dot_general(uv, q["bk"], tn, preferred_element_type=F32)
        s_scr[p] = (q["s0"] + jnp.where(bd, upd, 0.0)) * q["gam_last"]

        y = jnp.where(lo1, y_st[0:l], y_st[l:2 * l])
        mean = half_sum(y) * (1.0 / hd)
        yc = y - mean
        var = half_sum(yc * yc) * (1.0 / hd)
        y = yc * lax.rsqrt(var + RW_GN_EPS) * lnw_ref[:, sl] + lnb_ref[:, sl]
        y_ref[:, sl] = ((y + q["bonus"]) * g_ref[:, sl].astype(F32)).astype(y_ref.dtype)


def _rwkv_core(r, k, v, lw, a, g, k_k, k_a, r_k, ln_w, ln_b, hd, chunk=64):
    b, s, d = r.shape
    rowv = lambda t: t.reshape(1, d).astype(F32)
    tok = pl.BlockSpec((None, chunk, d), lambda bi, i: (bi, i, 0))
    full = pl.BlockSpec((1, d), lambda bi, i: (0, 0))
    return pl.pallas_call(
        functools.partial(_rwkv_kernel, hd=hd),
        grid=(b, s // chunk),
        in_specs=[tok] * 6 + [full] * 5,
        out_specs=tok,
        out_shape=jax.ShapeDtypeStruct((b, s, d), BF16),
        scratch_shapes=[pltpu.VMEM((d // (2 * hd), 2 * hd, 2 * hd), F32)],
        compiler_params=_cparams("parallel", "arbitrary"),
        name="rwkv_core",
    )(r, k, v, lw, a, g, rowv(k_k), rowv(k_a), rowv(r_k), rowv(ln_w), rowv(ln_b))


def _rwkv_layer(x, nw, sc, sh, g_mod, w_in, mu, w0, w_lora_b, a0, a_lora_b, g_lora_b, k_k, k_a, r_k,
                ln_w, ln_b, w_out, hd):
    r, k, v, lw, a, g = _rwkv_inproj(x, nw, sc, sh, w_in, mu, w0, w_lora_b, a0, a_lora_b, g_lora_b)
    y = _rwkv_core(r, k, v, lw, a, g, k_k, k_a, r_k, ln_w, ln_b, hd)
    return _outproj_residual(x, y, g_mod, w_out.astype(BF16))


MOE_TILE = 1024
MOE_ROWS = 128
MOE_SPECIALISED_ROWS = (240, 272, 304, 336)
MOE_GROUP = 2


def _router_kernel(x_ref, nw_ref, sc_ref, sh_ref, wr_ref, hb_ref, comb_ref, pos_ref, cnt_ref, *, ne):
    t = x_ref.shape[0]
    h = _norm_mod(x_ref[...], nw_ref[...], sc_ref[...], sh_ref[...])
    hb_ref[...] = h.astype(hb_ref.dtype)
    lane = lax.broadcasted_iota(jnp.int32, (t, LANES), 1)
    logits = jnp.where(lane < ne, _dot3(h, wr_ref[...]), NEG_BIG)
    m1 = jnp.max(logits, axis=-1, keepdims=True)
    i1 = jnp.min(jnp.where(logits == m1, lane, LANES), axis=-1, keepdims=True)
    sel1 = lane == i1
    rest = jnp.where(sel1, NEG_BIG, logits)
    m2 = jnp.max(rest, axis=-1, keepdims=True)
    i2 = jnp.min(jnp.where(rest == m2, lane, LANES), axis=-1, keepdims=True)
    sel2 = lane == i2
    e2 = jnp.exp(m2 - m1)
    w1 = 1.0 / (1.0 + e2)
    comb = jnp.where(sel1, w1, 0.0) + jnp.where(sel2, e2 * w1, 0.0)
    sel = jnp.where(sel1 | sel2, 1.0, 0.0)
    below = jnp.where(_tri(t, "strict_lower"), 1.0, 0.0).astype(BF16)
    slot = jnp.dot(below, sel.astype(BF16), preferred_element_type=F32)
    comb_ref[...] = comb.T
    pos_ref[...] = jnp.where(sel > 0.0, slot, -1.0).T
    cnt = jnp.sum(sel, axis=0, keepdims=True).astype(jnp.int32)
    cnt_ref[...] = jnp.broadcast_to(cnt, cnt_ref.shape)


def _moe_router(x, nw, sc, sh, w_router, tile):
    b, s, d = x.shape
    ne = w_router.shape[1]
    nt = s // tile
    vec = pl.BlockSpec((None, 1, d), lambda bi, i: (bi, 0, 0))
    return pl.pallas_call(
        functools.partial(_router_kernel, ne=ne),
        grid=(b, nt),
        in_specs=[pl.BlockSpec((None, tile, d), lambda bi, i: (bi, i, 0)),
                  pl.BlockSpec((1, d), lambda bi, i: (0, 0)), vec, vec,
                  pl.BlockSpec((d, LANES), lambda bi, i: (0, 0))],
        out_specs=[pl.BlockSpec((None, tile, d), lambda bi, i: (bi, i, 0)),
                   pl.BlockSpec((None, LANES, tile), lambda bi, i: (bi, 0, i)),
                   pl.BlockSpec((None, LANES, tile), lambda bi, i: (bi, 0, i)),
                   pl.BlockSpec((None, None, 8, LANES), lambda bi, i: (bi, i, 0, 0))],
        out_shape=[jax.ShapeDtypeStruct((b, s, d), BF16),
                   jax.ShapeDtypeStruct((b, LANES, s), F32),
                   jax.ShapeDtypeStruct((b, LANES, s), F32),
                   jax.ShapeDtypeStruct((b, nt, 8, LANES), jnp.int32)],
        compiler_params=_cparams("parallel", "parallel"),
        name="moe_router",
    )(x, nw.reshape(1, d), sc, sh, _pad_cols(w_router, LANES))


def _expert_kernel(cnt_ref, x_ref, hb_ref, pos_ref, comb_ref, g_ref, wg_ref, wu_ref, wd_ref, *rest,
                   rows, ne, tile, nsub, final_norm):
    fnw_ref = rest[0] if final_norm else None
    o_ref, hg_scr, yacc_scr = rest[-3:]
    bi, gi, e, f = (pl.program_id(a) for a in range(4))
    nf = pl.num_programs(3)
    ntiles = pl.num_programs(1) * nsub

    def run_tile(sub, carry):
        cnt = cnt_ref[(bi * ntiles + gi * nsub + sub) * ne + e]
        nblk = lax.shift_right_logical(cnt + (rows - 1), int(math.log2(rows)))
        tok = pl.ds(pl.multiple_of(sub * tile, tile), tile)

        @pl.when((e == 0) & (f == 0))
        def _():
            o_ref[tok, :] = x_ref[tok, :]

        def run_rows(row0, m):
            rs = pl.ds(row0, m)

            def one_hot():
                slot = pos_ref[sub, pl.ds(e, 1), :]
                want = (lax.broadcasted_iota(jnp.int32, (m, 1), 0) + row0).astype(F32)
                return slot == want

            @pl.when(f == 0)
            def _():
                p = jnp.where(one_hot(), 1.0, 0.0).astype(BF16)
                hg_scr[sub, rs, :] = jnp.dot(p, hb_ref[tok, :],
                                             preferred_element_type=F32).astype(BF16)

            hg = hg_scr[sub, rs, :]
            gate = jnp.dot(hg, wg_ref[...], preferred_element_type=F32)
            up = jnp.dot(hg, wu_ref[...], preferred_element_type=F32)
            act = (gate * _sigmoid(gate) * up).astype(BF16)
            part = jnp.dot(act, wd_ref[...], preferred_element_type=F32)

            @pl.when(f == 0)
            def _():
                yacc_scr[sub, rs, :] = part

            @pl.when(f > 0)
            def _():
                yacc_scr[sub, rs, :] += part

            @pl.when(f == nf - 1)
            def _():
                hit = one_hot()
                wrow = comb_ref[sub, pl.ds(e, 1), :]
                wgt = jnp.sum(jnp.where(hit, wrow, 0.0), axis=-1, keepdims=True)
                yw = (yacc_scr[sub, rs, :] * wgt * g_ref[...]).astype(BF16)
                p = jnp.where(hit, 1.0, 0.0).astype(BF16)
                o_ref[tok, :] += lax.dot_general(p, yw, (((0,), (0,)), ((), ())),
                                                 preferred_element_type=F32)

        lo = 0
        for m in MOE_SPECIALISED_ROWS:
            @pl.when((cnt > lo) & (cnt <= m))
            def _():
                run_rows(0, m)
            lo = m

        @pl.when(cnt > lo)
        def _():
            def body(j, c):
                run_rows(pl.multiple_of(j * rows, rows), rows)
                return c
            lax.fori_loop(0, nblk, body, 0)

        if final_norm:
            @pl.when((e == ne - 1) & (f == nf - 1))
            def _():
                xo = o_ref[tok, :]
                ms = jnp.mean(xo * xo, axis=-1, keepdims=True)
                o_ref[tok, :] = xo * lax.rsqrt(ms + RMS_EPS) * fnw_ref[...]

        return carry

    lax.fori_loop(0, nsub, run_tile, 0)


def _moe_experts(x, hb, pos, comb, counts, g, w_gu_bf16, w_down_bf16, layer, tile, nsub, rows,
                 final_norm_w=None, tf=896):
    b, s, d = x.shape
    final_norm = final_norm_w is not None
    extra_specs = [pl.BlockSpec((1, d), lambda bi, i, e, f, c: (0, 0))] if final_norm else []
    extra_args = [final_norm_w.reshape(1, d).astype(F32)] if final_norm else []
    _, ne, dff, _ = w_down_bf16.shape
    nf = dff // tf
    grp = tile * nsub
    once = pl.Buffered(1)
    grid_spec = pltpu.PrefetchScalarGridSpec(
        num_scalar_prefetch=1,
        grid=(b, s // grp, ne, nf),
        in_specs=[
            pl.BlockSpec((None, grp, d), lambda bi, i, e, f, c: (bi, i, 0), pipeline_mode=once),
            pl.BlockSpec((None, grp, d), lambda bi, i, e, f, c: (bi, i, 0), pipeline_mode=once),
            pl.BlockSpec((None, nsub, 8, tile), lambda bi, i, e, f, c: (bi, i, 0, 0)),
            pl.BlockSpec((None, nsub, 8, tile), lambda bi, i, e, f, c: (bi, i, 0, 0)),
            pl.BlockSpec((None, 1, d), lambda bi, i, e, f, c: (bi, 0, 0)),
            pl.BlockSpec((None, None, d, tf), lambda bi, i, e, f, c: (layer, e, 0, f)),
            pl.BlockSpec((None, None, d, tf), lambda bi, i, e, f, c: (layer, e, 0, f + nf)),
            pl.BlockSpec((None, None, tf, d), lambda bi, i, e, f, c: (layer, e, f, 0)),
        ] + extra_specs,
        out_specs=pl.BlockSpec((None, grp, d), lambda bi, i, e, f, c: (bi, i, 0)),
        scratch_shapes=[pltpu.VMEM((nsub, tile, d), BF16), pltpu.VMEM((nsub, tile, d), F32)],
    )
    return pl.pallas_call(
        functools.partial(_expert_kernel, rows=rows, ne=ne, tile=tile, nsub=nsub,
                          final_norm=final_norm),
        grid_spec=grid_spec,
        out_shape=jax.ShapeDtypeStruct((b, s, d), F32),
        input_output_aliases={1: 0},
        compiler_params=_cparams("parallel", "parallel", "arbitrary", "arbitrary"),
        name="moe_experts",
    )(counts, x, hb, pos, comb, g, w_gu_bf16, w_gu_bf16, w_down_bf16, *extra_args)


def _moe_layer(x, nw, sc, sh, g, w_router, w_gu_bf16, w_down_bf16, layer, final_norm_w=None):
    b, s, _ = x.shape
    ne = w_router.shape[1]
    tile = min(MOE_TILE, s)
    nsub = min(MOE_GROUP, s // tile)
    hb, comb_t, pos_t, counts = _moe_router(x, nw, sc, sh, w_router, tile)
    counts = counts[:, :, 0, :ne].reshape(-1)
    by_tile = lambda a: jnp.swapaxes(a[:, :8].reshape(b, 8, s // tile, tile), 1, 2)
    return _moe_experts(x, hb, by_tile(pos_t), by_tile(comb_t), counts, g, w_gu_bf16, w_down_bf16,
                        layer, tile, nsub, MOE_ROWS, final_norm_w)


def _final_norm_kernel(x_ref, w_ref, o_ref):
    x = x_ref[...]
    ms = jnp.mean(x * x, axis=-1, keepdims=True)
    o_ref[...] = x * lax.rsqrt(ms + RMS_EPS) * w_ref[...]


def _final_norm(x, w, tm=1024):
    b, s, d = x.shape
    return pl.pallas_call(
        _final_norm_kernel,
        grid=(b, s // tm),
        in_specs=[pl.BlockSpec((None, tm, d), lambda bi, i: (bi, i, 0)),
                  pl.BlockSpec((1, d), lambda bi, i: (0, 0))],
        out_specs=pl.BlockSpec((None, tm, d), lambda bi, i: (bi, i, 0)),
        out_shape=jax.ShapeDtypeStruct((b, s, d), F32),
        compiler_params=_cparams("parallel", "parallel"),
        name="final_norm",
    )(x, w.reshape(1, d))


ML_HEADS = 4
SSD_GROUPS, SSD_HEADDIM, SSD_STATE = 4, 64, 128
RW_HEADDIM = 64


def kernel(x, c, ada_w, ada_b, norm_w, final_norm_w, ml_w_in, ml_gate_b, ml_norm_w, ml_w_out, ssd_w_in, ssd_conv_w, ssd_conv_b, ssd_dt_bias, ssd_a_log, ssd_d, ssd_norm_w, ssd_w_out, rw_w_in, rw_mu, rw_w0, rw_w_lora_b, rw_a0, rw_a_lora_b, rw_g_lora_b, rw_k_k, rw_k_a, rw_r_k, rw_ln_w, rw_ln_b, rw_w_out, lru_w_in, lru_conv_w, lru_conv_b, lru_gx_w, lru_gx_b, lru_ga_w, lru_ga_b, lru_lambda, lru_w_out, ffn_w_gu, ffn_w_down, moe_router, moe_w_gu, moe_w_down):
    depth = ada_w.shape[0]
    mod = _ada_modulation(c, ada_w, ada_b)
    moe_gu_b, moe_down_b = moe_w_gu.astype(BF16), moe_w_down.astype(BF16)
    for layer in range(depth):
        sh_t, sc_t, g_t, sh_c, sc_c, g_c = [m[:, None, :] for m in jnp.split(mod[layer], 6, axis=-1)]
        kind, j = layer % 4, layer // 4
        nw_t, nw_c = norm_w[layer, 0], norm_w[layer, 1]
        if kind == 0:
            x = _mlstm_layer(x, nw_t, sc_t, sh_t, g_t, ml_w_in[j], ml_gate_b[j], ml_norm_w[j],
                             ml_w_out[j], ML_HEADS)
        elif kind == 1:
            x = _ssd_layer(x, nw_t, sc_t, sh_t, g_t, ssd_w_in[j], ssd_conv_w[j], ssd_conv_b[j],
                           ssd_dt_bias[j], ssd_a_log[j], ssd_d[j], ssd_norm_w[j], ssd_w_out[j],
                           SSD_GROUPS, SSD_HEADDIM, SSD_STATE)
        elif kind == 2:
            x = _rwkv_layer(x, nw_t, sc_t, sh_t, g_t, rw_w_in[j], rw_mu[j], rw_w0[j], rw_w_lora_b[j],
                            rw_a0[j], rw_a_lora_b[j], rw_g_lora_b[j], rw_k_k[j], rw_k_a[j], rw_r_k[j],
                            rw_ln_w[j], rw_ln_b[j], rw_w_out[j], RW_HEADDIM)
        else:
            x = _rglru_layer(x, nw_t, sc_t, sh_t, g_t, lru_w_in[j], lru_conv_w[j], lru_conv_b[j],
                             lru_gx_w[j], lru_gx_b[j], lru_ga_w[j], lru_ga_b[j], lru_lambda[j],
                             lru_w_out[j])
        if layer % 2 == 0:
            x = _ffn_dense(x, nw_c, sc_c, sh_c, g_c, ffn_w_gu[layer // 2].astype(BF16),
                           ffn_w_down[layer // 2].astype(BF16))
        else:
            closing = final_norm_w if layer == depth - 1 else None
            x = _moe_layer(x, nw_c, sc_c, sh_c, g_c, moe_router[layer // 2], moe_gu_b, moe_down_b,
                           layer // 2, closing)
    if depth % 2 == 0:
        return x
    return _final_norm(x, final_norm_w)
```

```python
import functools
import math

import jax
import jax.numpy as jnp
from jax import lax
from jax.experimental import pallas as pl
from jax.experimental.pallas import tpu as pltpu

F32 = jnp.float32
BF16 = jnp.bfloat16

RMS_EPS = 1e-6
LANES = 128
VMEM_LIMIT_BYTES = 56 * 1024 * 1024


def _cparams(*sem):
    return pltpu.CompilerParams(dimension_semantics=sem, vmem_limit_bytes=VMEM_LIMIT_BYTES)


def _sigmoid(x):
    return 1.0 / (1.0 + jnp.exp(-x))


def _softplus(x):
    return jnp.maximum(x, 0.0) + jnp.log(1.0 + jnp.exp(-jnp.abs(x)))


def _bdot(a, b):
    return jnp.dot(a.astype(BF16), b.astype(BF16), preferred_element_type=F32)


def _split_bf16(a):
    hi = a.astype(BF16)
    lo = (a - hi.astype(F32)).astype(BF16)
    return hi, lo


def _dot3(a, b):
    a_hi, a_lo = _split_bf16(a)
    b_hi, b_lo = _split_bf16(b)
    d = functools.partial(jnp.dot, preferred_element_type=F32)
    return d(a_hi, b_hi) + d(a_lo, b_hi) + d(a_hi, b_lo)


def _norm_mod(x, nw, sc, sh):
    ms = jnp.mean(x * x, axis=-1, keepdims=True)
    y = x * lax.rsqrt(ms + RMS_EPS) * nw
    return y * (1.0 + sc) + sh


def _pad_cols(w, n):
    return jnp.pad(w, ((0, 0), (0, n - w.shape[1])))


def _ada_kernel(c_ref, w_ref, b_ref, o_ref):
    c = c_ref[...]
    cond = c * _sigmoid(c)
    o_ref[...] = _dot3(cond, w_ref[...]) + b_ref[...]


def _ada_modulation(c, ada_w, ada_b):
    depth, d, n = ada_w.shape
    b = c.shape[0]
    rows = 8
    cp = jnp.pad(c, ((0, rows - b), (0, 0)))
    tn = 1024
    out = pl.pallas_call(
        _ada_kernel,
        grid=(depth, n // tn),
        in_specs=[
            pl.BlockSpec((rows, d), lambda l, j: (0, 0)),
            pl.BlockSpec((None, d, tn), lambda l, j: (l, 0, j)),
            pl.BlockSpec((None, 1, tn), lambda l, j: (l, 0, j)),
        ],
        out_specs=pl.BlockSpec((None, rows, tn), lambda l, j: (l, 0, j)),
        out_shape=jax.ShapeDtypeStruct((depth, rows, n), F32),
        compiler_params=_cparams("parallel", "parallel"),
        name="ada_modulation",
    )(cp, ada_w, ada_b.reshape(depth, 1, n))
    return out[:, :b]


def _nmm_kernel(x_ref, nw_ref, sc_ref, sh_ref, w_ref, *o_refs, segs):
    hb = _norm_mod(x_ref[...], nw_ref[...], sc_ref[...], sh_ref[...]).astype(BF16)
    off = 0
    for o_ref, n in zip(o_refs, segs):
        o_ref[...] = jnp.dot(hb, w_ref[:, off:off + n],
                             preferred_element_type=F32).astype(o_ref.dtype)
        off += n


def _norm_mod_matmul(x, nw, sc, sh, w_bf16, segs, dtypes, tm=512):
    b, s, d = x.shape
    ntot = w_bf16.shape[1]
    assert sum(segs) == ntot and s % tm == 0
    vec = pl.BlockSpec((None, 1, d), lambda bi, i: (bi, 0, 0))
    return pl.pallas_call(
        functools.partial(_nmm_kernel, segs=tuple(segs)),
        grid=(b, s // tm),
        in_specs=[
            pl.BlockSpec((None, tm, d), lambda bi, i: (bi, i, 0)),
            pl.BlockSpec((1, d), lambda bi, i: (0, 0)),
            vec, vec,
            pl.BlockSpec((d, ntot), lambda bi, i: (0, 0)),
        ],
        out_specs=[pl.BlockSpec((None, tm, n), lambda bi, i: (bi, i, 0)) for n in segs],
        out_shape=[jax.ShapeDtypeStruct((b, s, n), dt) for n, dt in zip(segs, dtypes)],
        compiler_params=_cparams("parallel", "parallel"),
        name="norm_mod_matmul",
    )(x, nw.reshape(1, d), sc, sh, w_bf16)


def _outproj_kernel(x_ref, y_ref, g_ref, w_ref, o_ref):
    o_ref[...] = x_ref[...] + g_ref[...] * jnp.dot(
        y_ref[...], w_ref[...], preferred_element_type=F32)


def _outproj_residual(x, y, g, w_bf16, tm=512):
    b, s, d = x.shape
    k = y.shape[-1]
    return pl.pallas_call(
        _outproj_kernel,
        grid=(b, s // tm),
        in_specs=[
            pl.BlockSpec((None, tm, d), lambda bi, i: (bi, i, 0)),
            pl.BlockSpec((None, tm, k), lambda bi, i: (bi, i, 0)),
            pl.BlockSpec((None, 1, d), lambda bi, i: (bi, 0, 0)),
            pl.BlockSpec((k, d), lambda bi, i: (0, 0)),
        ],
        out_specs=pl.BlockSpec((None, tm, d), lambda bi, i: (bi, i, 0)),
        out_shape=jax.ShapeDtypeStruct((b, s, d), F32),
        compiler_params=_cparams("parallel", "parallel"),
        name="outproj_residual",
    )(x, y, g, w_bf16)


def _ffn_kernel(x_ref, nw_ref, sc_ref, sh_ref, g_ref, wg_ref, wu_ref, wd_ref, o_ref,
                h_scr, acc_scr):
    f = pl.program_id(2)

    @pl.when(f == 0)
    def _():
        h_scr[...] = _norm_mod(x_ref[...], nw_ref[...], sc_ref[...], sh_ref[...]).astype(BF16)
        acc_scr[...] = jnp.zeros_like(acc_scr)

    hb = h_scr[...]
    gate = jnp.dot(hb, wg_ref[...], preferred_element_type=F32)
    up = jnp.dot(hb, wu_ref[...], preferred_element_type=F32)
    act = (gate * _sigmoid(gate) * up).astype(BF16)
    acc_scr[...] += jnp.dot(act, wd_ref[...], preferred_element_type=F32)

    @pl.when(f == pl.num_programs(2) - 1)
    def _():
        o_ref[...] = x_ref[...] + g_ref[...] * acc_scr[...]


def _ffn_dense(x, nw, sc, sh, g, w_gu_bf16, w_down_bf16, tm=1024, tf=512):
    b, s, d = x.shape
    dff = w_down_bf16.shape[0]
    nf = dff // tf
    vec = pl.BlockSpec((None, 1, d), lambda bi, i, f: (bi, 0, 0))
    return pl.pallas_call(
        _ffn_kernel,
        grid=(b, s // tm, nf),
        in_specs=[
            pl.BlockSpec((None, tm, d), lambda bi, i, f: (bi, i, 0)),
            pl.BlockSpec((1, d), lambda bi, i, f: (0, 0)),
            vec, vec, vec,
            pl.BlockSpec((d, tf), lambda bi, i, f: (0, f)),
            pl.BlockSpec((d, tf), lambda bi, i, f: (0, f + nf)),
            pl.BlockSpec((tf, d), lambda bi, i, f: (f, 0)),
        ],
        out_specs=pl.BlockSpec((None, tm, d), lambda bi, i, f: (bi, i, 0)),
        out_shape=jax.ShapeDtypeStruct((b, s, d), F32),
        scratch_shapes=[pltpu.VMEM((tm, d), BF16), pltpu.VMEM((tm, d), F32)],
        input_output_aliases={0: 0},
        compiler_params=_cparams("parallel", "parallel", "arbitrary"),
        name="ffn_dense",
    )(x, nw.reshape(1, d), sc, sh, g, w_gu_bf16, w_gu_bf16, w_down_bf16)


CONV_W = 4
TAIL = 8


def _causal_conv(pad_ref, x, w_ref, b_ref, first):
    l = x.shape[0]

    @pl.when(first)
    def _():
        pad_ref[0:TAIL, :] = jnp.zeros((TAIL, x.shape[1]), F32)

    pad_ref[TAIL:2 * TAIL, :] = x[0:TAIL, :]
    y = b_ref[...] + w_ref[CONV_W - 1:CONV_W, :] * x
    head = y[0:TAIL, :]
    for k in range(CONV_W - 1):
        shift = CONV_W - 1 - k
        y = y + w_ref[k:k + 1, :] * pltpu.roll(x, shift, axis=0)
        head = head + w_ref[k:k + 1, :] * pad_ref[TAIL - shift:2 * TAIL - shift, :]
    pad_ref[0:TAIL, :] = x[l - TAIL:, :]
    return jnp.concatenate([head, y[TAIL:, :]], axis=0)


LRU_C = 8.0


def _gelu_tanh(x):
    return 0.5 * x * (1.0 + jnp.tanh(math.sqrt(2.0 / math.pi) * (x + 0.044715 * (x * x * x))))


def _lru_kernel(xbr_ref, ybr_ref, cw_ref, cb_ref, gxw_ref, gxb_ref, gaw_ref, gab_ref, lam_ref,
                o_ref, pad_scr, h_scr):
    i = pl.program_id(1)
    l, w = xbr_ref.shape
    nblk, blk = gxw_ref.shape[0], gxw_ref.shape[1]

    @pl.when(i == 0)
    def _():
        h_scr[...] = jnp.zeros_like(h_scr)

    xb = _causal_conv(pad_scr, xbr_ref[...], cw_ref, cb_ref, i == 0)
    xbb = xb.astype(BF16)
    gx = jnp.concatenate([jnp.dot(xbb[:, n * blk:(n + 1) * blk], gxw_ref[n],
                                  preferred_element_type=F32) for n in range(nblk)], axis=-1)
    ga = jnp.concatenate([jnp.dot(xbb[:, n * blk:(n + 1) * blk], gaw_ref[n],
                                  preferred_element_type=F32) for n in range(nblk)], axis=-1)
    gate_x = _sigmoid(gx + gxb_ref[...])
    gate_a = _sigmoid(ga + gab_ref[...])
    log_a = LRU_C * gate_a * (-_softplus(-lam_ref[...]))
    a = jnp.exp(log_a)
    u = jnp.sqrt(1.0 - jnp.exp(2.0 * log_a)) * gate_x * xb

    row = lax.broadcasted_iota(jnp.int32, (l, 1), 0)
    d = 1
    while d < l:
        if d < TAIL:
            a_sh = pltpu.roll(a, d, axis=0)
            u_sh = pltpu.roll(u, d, axis=0)
            valid = row >= d
            u = jnp.where(valid, a * u_sh + u, u)
            a = jnp.where(valid, a * a_sh, a)
        else:
            u = jnp.concatenate([u[:d], a[d:] * u[:l - d] + u[d:]], axis=0)
            a = jnp.concatenate([a[:d], a[d:] * a[:l - d]], axis=0)
        d *= 2
    hs = a * h_scr[0:1, :] + u
    h_scr[...] = jnp.broadcast_to(hs[l - 1:l, :], h_scr.shape)
    o_ref[...] = (_gelu_tanh(ybr_ref[...].astype(F32)) * hs).astype(o_ref.dtype)


def _rglru_core(x_br, y_br, conv_w, conv_b, gx_w, gx_b, ga_w, ga_b, lam, chunk=256):
    b, s, w = x_br.shape
    nblk, blk, _ = gx_w.shape
    row = lambda a: a.reshape(1, w).astype(F32)
    full2 = lambda shape: pl.BlockSpec(shape, lambda bi, i: (0, 0))
    full3 = lambda shape: pl.BlockSpec(shape, lambda bi, i: (0, 0, 0))
    return pl.pallas_call(
        _lru_kernel,
        grid=(b, s // chunk),
        in_specs=[
            pl.BlockSpec((None, chunk, w), lambda bi, i: (bi, i, 0)),
            pl.BlockSpec((None, chunk, w), lambda bi, i: (bi, i, 0)),
            full2((CONV_W, w)), full2((1, w)),
            full3((nblk, blk, blk)), full2((1, w)),
            full3((nblk, blk, blk)), full2((1, w)),
            full2((1, w)),
        ],
        out_specs=pl.BlockSpec((None, chunk, w), lambda bi, i: (bi, i, 0)),
        out_shape=jax.ShapeDtypeStruct((b, s, w), BF16),
        scratch_shapes=[pltpu.VMEM((2 * TAIL, w), F32), pltpu.VMEM((8, w), F32)],
        compiler_params=_cparams("parallel", "arbitrary"),
        name="rglru_core",
    )(x_br, y_br, conv_w, row(conv_b), gx_w.astype(BF16), row(gx_b), ga_w.astype(BF16), row(ga_b),
      row(lam))


def _rglru_layer(x, nw, sc, sh, g, w_in, conv_w, conv_b, gx_w, gx_b, ga_w, ga_b, lam, w_out):
    w = w_in.shape[1] // 2
    y_br, x_br = _norm_mod_matmul(x, nw, sc, sh, w_in.astype(BF16), [w, w], [BF16, F32])
    out = _rglru_core(x_br, y_br, conv_w, conv_b, gx_w, gx_b, ga_w, ga_b, lam)
    return _outproj_residual(x, out, g, w_out.astype(BF16))


NEG_BIG = -1e30


def _split3(a):
    hi = a.astype(BF16)
    r = a - hi.astype(F32)
    mid = r.astype(BF16)
    lo = (r - mid.astype(F32)).astype(BF16)
    return hi, mid, lo


def _tri(l, kind):
    r = lax.broadcasted_iota(jnp.int32, (l, l), 0)
    c = lax.broadcasted_iota(jnp.int32, (l, l), 1)
    m = {"lower": r >= c, "strict_lower": r > c, "upper": r <= c}[kind]
    return m


def _cumsum_rows(a):
    t = jnp.where(_tri(a.shape[0], "lower"), 1.0, 0.0).astype(BF16)
    return sum(jnp.dot(t, p, preferred_element_type=F32) for p in _split3(a))


def _cumsum_lanes(a):
    t = jnp.where(_tri(a.shape[1], "upper"), 1.0, 0.0).astype(BF16)
    return sum(jnp.dot(p, t, preferred_element_type=F32) for p in _split3(a))


def _mlstm_kernel(q_ref, k_ref, v_ref, o_ref, gc_ref, gr_ref, gbr_ref, gbc_ref, nw_ref, y_ref,
                  c_scr, n_scr, m_scr, *, heads, dk, dv):
    i = pl.program_id(1)
    l = q_ref.shape[0]

    @pl.when(i == 0)
    def _():
        c_scr[...] = jnp.zeros_like(c_scr)
        n_scr[...] = jnp.zeros_like(n_scr)
        m_scr[...] = jnp.zeros_like(m_scr)

    gc = gc_ref[...] + gbr_ref[...]
    gr = gr_ref[...] + gbc_ref[...]
    li_col = gc[:, 0:heads]
    li_row = gr[0:heads, :]
    g_col = _cumsum_rows(-_softplus(-gc))[:, heads:2 * heads]
    g_row = _cumsum_lanes(-_softplus(-gr))[heads:2 * heads, :]
    causal = _tri(l, "lower")
    scale = dk ** -0.5

    st = []
    for h in range(heads):
        a_col = g_col[:, h:h + 1]
        b_row = li_row[h:h + 1, :] - g_row[h:h + 1, :]
        m_prev = m_scr[h, 0:1, 0:1]
        logd = jnp.where(causal, a_col + b_row, NEG_BIG)
        log_inter = a_col + m_prev
        m_row = jnp.maximum(jnp.max(logd, axis=-1, keepdims=True), log_inter)
        qs = (q_ref[:, h * dk:(h + 1) * dk].astype(F32) * scale).astype(BF16)
        k = k_ref[:, h * dk:(h + 1) * dk]
        c_mat = c_scr[h]
        st.append(dict(
            a_col=a_col, b_col=li_col[:, h:h + 1] - a_col, m_prev=m_prev, m_row=m_row, qs=qs, k=k,
            c_mat=c_mat, e_inter=jnp.exp(log_inter - m_row),
            s=lax.dot_general(qs, k, (((1,), (1,)), ((), ())), preferred_element_type=F32)
            * jnp.exp(logd - m_row),
            qc=jnp.dot(qs, c_mat.astype(BF16), preferred_element_type=F32)))

    for h, t in enumerate(st):
        v = v_ref[:, h * dv:(h + 1) * dv]
        n_vec = n_scr[h, 0:1, :]
        num = jnp.dot(t["s"].astype(BF16), v, preferred_element_type=F32) + t["e_inter"] * t["qc"]
        den = (jnp.sum(t["s"], axis=-1, keepdims=True)
               + t["e_inter"] * jnp.sum(t["qs"].astype(F32) * n_vec, axis=-1, keepdims=True))
        denom = jnp.maximum(jnp.abs(den), jnp.exp(-t["m_row"]))
        hh = num / denom
        hh = hh * lax.rsqrt(jnp.mean(hh * hh, axis=-1, keepdims=True) + RMS_EPS)
        gate = _sigmoid(o_ref[:, h * dv:(h + 1) * dv].astype(F32))
        y_ref[:, h * dv:(h + 1) * dv] = (hh * nw_ref[:, h * dv:(h + 1) * dv] * gate).astype(y_ref.dtype)

    for h, t in enumerate(st):
        v = v_ref[:, h * dv:(h + 1) * dv]
        n_vec = n_scr[h, 0:1, :]
        g_last = t["a_col"][l - 1:l, :]
        log_w = g_last + t["b_col"]
        m_new = jnp.maximum(g_last + t["m_prev"], jnp.max(log_w, axis=0, keepdims=True))
        kw = t["k"].astype(F32) * jnp.exp(log_w - m_new)
        decay = jnp.exp(g_last + t["m_prev"] - m_new)
        c_scr[h] = decay * t["c_mat"] + lax.dot_general(
            kw.astype(BF16), v, (((0,), (0,)), ((), ())), preferred_element_type=F32)
        n_scr[h, 0:1, :] = decay * n_vec + jnp.sum(kw, axis=0, keepdims=True)
        m_scr[h] = jnp.broadcast_to(m_new, m_scr.shape[1:])


def _mlstm_core(q, k, v, o, gates, gate_b, norm_w, heads, chunk=128):
    b, s, hk = q.shape
    hv = v.shape[-1]
    dk, dv = hk // heads, hv // heads
    g2 = 2 * heads
    gates_t = jnp.swapaxes(gates, 1, 2)
    gb = gate_b.reshape(1, g2).astype(F32)
    tok = lambda n: pl.BlockSpec((None, chunk, n), lambda bi, i: (bi, i, 0))
    full = lambda shape: pl.BlockSpec(shape, lambda bi, i: (0, 0))
    return pl.pallas_call(
        functools.partial(_mlstm_kernel, heads=heads, dk=dk, dv=dv),
        grid=(b, s // chunk),
        in_specs=[tok(hk), tok(hk), tok(hv), tok(hv), tok(g2),
                  pl.BlockSpec((None, g2, chunk), lambda bi, i: (bi, 0, i)),
                  full((1, g2)), full((g2, 1)), full((1, hv))],
        out_specs=tok(hv),
        out_shape=jax.ShapeDtypeStruct((b, s, hv), BF16),
        scratch_shapes=[pltpu.VMEM((heads, dk, dv), F32), pltpu.VMEM((heads, 8, dk), F32),
                        pltpu.VMEM((heads, 8, LANES), F32)],
        compiler_params=_cparams("parallel", "arbitrary"),
        name="mlstm_core",
    )(q, k, v, o, gates, gates_t, gb, gb.reshape(g2, 1), norm_w.reshape(1, hv).astype(F32))


def _mlstm_layer(x, nw, sc, sh, g, w_in, gate_b, norm_w, w_out, heads):
    hv = w_out.shape[0]
    hk = (w_in.shape[1] - 2 * hv - 2 * heads) // 2
    w_pad = _pad_cols(w_in, 2 * hk + 2 * hv + LANES).astype(BF16)
    q, k, v, o, gates = _norm_mod_matmul(x, nw, sc, sh, w_pad, [hk, hk, hv, hv, LANES],
                                         [BF16, BF16, BF16, BF16, F32])
    y = _mlstm_core(q, k, v, o, gates[..., :2 * heads], gate_b, norm_w, heads)
    return _outproj_residual(x, y, g, w_out.astype(BF16))


def _pair_select(lo_half, a, b):
    return jnp.where(lo_half, a, b)


def _ssd_kernel(z_ref, xbc_ref, dtc_ref, dtr_ref, cw_ref, cb_ref, dbr_ref, dbc_ref, alr_ref, alc_ref,
                dsk_ref, nw_ref, y_ref, pad_scr, st_scr, *, groups, hpg, hd, ns):
    i = pl.program_id(1)
    l = z_ref.shape[0]
    inner = groups * hpg * hd
    gw = hpg * hd
    pair = 2 * hd

    @pl.when(i == 0)
    def _():
        st_scr[...] = jnp.zeros_like(st_scr)

    conv = _causal_conv(pad_scr, xbc_ref[...].astype(F32), cw_ref, cb_ref, i == 0)
    conv = conv * _sigmoid(conv)
    xs = conv[:, 0:inner]
    xs_b = xs.astype(BF16)
    bm = conv[:, inner:inner + groups * ns].astype(BF16)
    cm = conv[:, inner + groups * ns:inner + 2 * groups * ns].astype(BF16)

    dt_col = _softplus(dtc_ref[...] + dbr_ref[...])
    dt_row = _softplus(dtr_ref[...] + dbc_ref[...])
    acum_col = _cumsum_rows(dt_col * (-jnp.exp(alr_ref[...])))
    acum_row = _cumsum_lanes(dt_row * (-jnp.exp(alc_ref[...])))
    a_last = acum_col[l - 1:l, :]
    ea_col = jnp.exp(acum_col)
    ws_col = jnp.exp(a_last - acum_col) * dt_col
    ea_last = jnp.exp(a_last)

    causal = _tri(l, "lower")
    lo_half = lax.broadcasted_iota(jnp.int32, (1, pair), 1) < hd

    for g in range(groups):
        bg = bm[:, g * ns:(g + 1) * ns]
        cg = cm[:, g * ns:(g + 1) * ns]
        cb = lax.dot_general(cg, bg, (((1,), (1,)), ((), ())), preferred_element_type=F32)
        st = st_scr[g]
        y_inter = jnp.dot(cg, st.astype(BF16), preferred_element_type=F32)
        xw_parts, y_parts, dec_parts = [], [], []
        for p in range(hpg // 2):
            h0 = g * hpg + 2 * p
            c0 = g * gw + p * pair
            xp = xs_b[:, c0:c0 + pair]
            ys = []
            for h in (h0, h0 + 1):
                dec = jnp.exp(jnp.where(causal, acum_col[:, h:h + 1] - acum_row[h:h + 1, :], NEG_BIG))
                wts = (cb * dec * dt_row[h:h + 1, :]).astype(BF16)
                ys.append(jnp.dot(wts, xp, preferred_element_type=F32))
            sel = lambda a: _pair_select(lo_half, a[:, h0:h0 + 1], a[:, h0 + 1:h0 + 2])
            y_parts.append(_pair_select(lo_half, ys[0], ys[1])
                           + sel(ea_col) * y_inter[:, p * pair:(p + 1) * pair])
            xw_parts.append((xs[:, c0:c0 + pair] * sel(ws_col)).astype(BF16))
            dec_parts.append(sel(ea_last))
        xw = jnp.concatenate(xw_parts, axis=-1)
        st_scr[g] = jnp.concatenate(dec_parts, axis=-1) * st + lax.dot_general(
            bg, xw, (((0,), (0,)), ((), ())), preferred_element_type=F32)
        yg = jnp.concatenate(y_parts, axis=-1)
        sl = slice(g * gw, (g + 1) * gw)
        yg = yg + dsk_ref[:, sl] * xs[:, sl]
        zg = z_ref[:, sl].astype(F32)
        yg = yg * (zg * _sigmoid(zg))
        yg = yg * lax.rsqrt(jnp.mean(yg * yg, axis=-1, keepdims=True) + RMS_EPS)
        y_ref[:, sl] = (yg * nw_ref[:, sl]).astype(y_ref.dtype)


def _ssd_core(z, xbc, dt, conv_w, conv_b, dt_bias, a_log, d_skip, norm_w, groups, hd, ns, chunk=128):
    b, s, inner = z.shape
    heads = dt.shape[-1]
    hpg = heads // groups
    cdim = xbc.shape[-1]
    dt_t = jnp.swapaxes(dt, 1, 2)
    rowv = lambda a: a.reshape(1, -1).astype(F32)
    colv = lambda a: a.reshape(-1, 1).astype(F32)
    tok = lambda n: pl.BlockSpec((None, chunk, n), lambda bi, i: (bi, i, 0))
    full = lambda shape: pl.BlockSpec(shape, lambda bi, i: (0, 0))
    return pl.pallas_call(
        functools.partial(_ssd_kernel, groups=groups, hpg=hpg, hd=hd, ns=ns),
        grid=(b, s // chunk),
        in_specs=[tok(inner), tok(cdim), tok(heads),
                  pl.BlockSpec((None, heads, chunk), lambda bi, i: (bi, 0, i)),
                  full((CONV_W, cdim)), full((1, cdim)),
                  full((1, heads)), full((heads, 1)), full((1, heads)), full((heads, 1)),
                  full((1, inner)), full((1, inner))],
        out_specs=tok(inner),
        out_shape=jax.ShapeDtypeStruct((b, s, inner), BF16),
        scratch_shapes=[pltpu.VMEM((2 * TAIL, cdim), F32),
                        pltpu.VMEM((groups, ns, hpg * hd), F32)],
        compiler_params=_cparams("parallel", "arbitrary"),
        name="ssd_core",
    )(z, xbc, dt, dt_t, conv_w, rowv(conv_b), rowv(dt_bias), colv(dt_bias), rowv(a_log), colv(a_log),
      rowv(jnp.repeat(d_skip, hd)), rowv(norm_w))


def _ssd_layer(x, nw, sc, sh, g, w_in, conv_w, conv_b, dt_bias, a_log, d_skip, norm_w, w_out,
               groups, hd, ns):
    inner = w_out.shape[0]
    heads = inner // hd
    cdim = inner + 2 * groups * ns
    w_pad = _pad_cols(w_in, inner + cdim + LANES).astype(BF16)
    z, xbc, dt = _norm_mod_matmul(x, nw, sc, sh, w_pad, [inner, cdim, LANES], [BF16, BF16, F32])
    y = _ssd_core(z, xbc, dt[..., :heads], conv_w, conv_b, dt_bias, a_log, d_skip, norm_w,
                  groups, hd, ns)
    return _outproj_residual(x, y, g, w_out.astype(BF16))


def _rwkv_inproj_kernel(x_ref, nw_ref, sc_ref, sh_ref, mu_ref, wr_ref, wk_ref, wv_ref,
                        wdw_ref, wda_ref, wdg_ref, w0_ref, wlb_ref, a0_ref, alb_ref, glb_ref,
                        r_ref, k_ref, v_ref, lw_ref, a_ref, g_ref, last_scr):
    i = pl.program_id(1)
    tm = x_ref.shape[0]

    @pl.when(i == 0)
    def _():
        last_scr[...] = jnp.zeros_like(last_scr)

    h = _norm_mod(x_ref[...], nw_ref[...], sc_ref[...], sh_ref[...])
    row = lax.broadcasted_iota(jnp.int32, (tm, 1), 0)
    h_prev = jnp.where(row == 0, last_scr[0:1, :], pltpu.roll(h, 1, axis=0))
    last_scr[...] = jnp.broadcast_to(h[tm - 1:tm, :], last_scr.shape)
    xx = h_prev - h

    def proj(b, w_ref):
        xb = (h + xx * mu_ref[b:b + 1, :]).astype(BF16)
        return jnp.dot(xb, w_ref[...], preferred_element_type=F32)

    r_ref[...] = proj(0, wr_ref).astype(r_ref.dtype)
    k_ref[...] = proj(1, wk_ref).astype(k_ref.dtype)
    v_ref[...] = proj(2, wv_ref).astype(v_ref.dtype)
    dw = proj(3, wdw_ref)
    da = proj(4, wda_ref)
    dg = proj(5, wdg_ref)
    z = w0_ref[...] + _bdot(jnp.tanh(dw), wlb_ref[...])
    w_log = -_softplus(-z) - 0.5
    lw_ref[...] = -jnp.exp(w_log)
    a_ref[...] = _sigmoid(a0_ref[...] + _bdot(da, alb_ref[...])).astype(a_ref.dtype)
    g_ref[...] = _bdot(_sigmoid(dg), glb_ref[...]).astype(g_ref.dtype)


def _pad_rows(w, n):
    return jnp.pad(w, ((0, n - w.shape[0]), (0, 0)))


def _round_up(n, m):
    return -(-n // m) * m


def _rwkv_inproj(x, nw, sc, sh, w_in, mu, w0, w_lora_b, a0, a_lora_b, g_lora_b, tm=256):
    b, s, d = x.shape
    nl = [w_lora_b.shape[0], a_lora_b.shape[0], g_lora_b.shape[0]]
    nlp = [_round_up(n, LANES) for n in nl]
    offs = [0, d, 2 * d, 3 * d, 3 * d + nl[0], 3 * d + nl[0] + nl[1]]
    wr, wk, wv = (w_in[:, offs[j]:offs[j] + d].astype(BF16) for j in range(3))
    wl = [_pad_cols(w_in[:, offs[3 + j]:offs[3 + j] + nl[j]], nlp[j]).astype(BF16) for j in range(3)]
    lb = [_pad_rows(m, n).astype(BF16) for m, n in zip((w_lora_b, a_lora_b, g_lora_b), nlp)]
    rowv = lambda a: a.reshape(1, d).astype(F32)
    tok = pl.BlockSpec((None, tm, d), lambda bi, i: (bi, i, 0))
    vec = pl.BlockSpec((None, 1, d), lambda bi, i: (bi, 0, 0))
    full = lambda a: pl.BlockSpec(a.shape, lambda bi, i: (0, 0))
    ins = [x, rowv(nw), sc, sh, mu.astype(F32), wr, wk, wv, wl[0], wl[1], wl[2],
           rowv(w0), lb[0], rowv(a0), lb[1], lb[2]]
    specs = [tok, full(ins[1]), vec, vec] + [full(a) for a in ins[4:]]
    dts = [BF16, BF16, BF16, F32, BF16, BF16]
    return pl.pallas_call(
        _rwkv_inproj_kernel,
        grid=(b, s // tm),
        in_specs=specs,
        out_specs=[tok] * 6,
        out_shape=[jax.ShapeDtypeStruct((b, s, d), dt) for dt in dts],
        scratch_shapes=[pltpu.VMEM((8, d), F32)],
        compiler_params=_cparams("parallel", "arbitrary"),
        name="rwkv_inproj",
    )(*ins)


RW_GN_EPS = 64e-5


def _rwkv_kernel(r_ref, k_ref, v_ref, lw_ref, a_ref, g_ref, kk_ref, ka_ref, rk_ref, lnw_ref, lnb_ref,
                 y_ref, s_scr, *, hd):
    i = pl.program_id(1)
    l, d = r_ref.shape
    pair = 2 * hd
    npairs = d // pair

    @pl.when(i == 0)
    def _():
        s_scr[...] = jnp.zeros_like(s_scr)

    lw = lw_ref[...]
    cum = _cumsum_rows(lw)
    lo1 = lax.broadcasted_iota(jnp.int32, (1, pair), 1) < hd
    r2 = lax.broadcasted_iota(jnp.int32, (2 * l, 1), 0) < l
    c2 = lax.broadcasted_iota(jnp.int32, (1, pair), 1) < hd
    stack_mask = r2 == c2
    rr = lax.broadcasted_iota(jnp.int32, (2 * l, 2 * l), 0)
    cc = lax.broadcasted_iota(jnp.int32, (2 * l, 2 * l), 1)
    same = (rr < l) == (cc < l)
    strict = same & (rr > cc)
    incl = same & (rr >= cc)
    kr = lax.broadcasted_iota(jnp.int32, (pair, pair), 0) < hd
    kc = lax.broadcasted_iota(jnp.int32, (pair, pair), 1) < hd
    bd = kr == kc

    def half_sum(t):
        s_lo = jnp.sum(jnp.where(lo1, t, 0.0), axis=-1, keepdims=True)
        s_hi = jnp.sum(jnp.where(lo1, 0.0, t), axis=-1, keepdims=True)
        return jnp.where(lo1, s_lo, s_hi)

    def stack(t, masked):
        t2 = jnp.concatenate([t, t], axis=0)
        return jnp.where(stack_mask, t2, 0.0) if masked else t2

    nt = (((1,), (1,)), ((), ()))
    tn = (((0,), (0,)), ((), ()))
    pairs = range(npairs)
    sls = [slice(p * pair, (p + 1) * pair) for p in pairs]
    st = []
    for p in pairs:
        sl = sls[p]
        r = r_ref[:, sl].astype(F32)
        k = k_ref[:, sl].astype(F32)
        v = v_ref[:, sl].astype(F32)
        a = a_ref[:, sl].astype(F32)
        cm = cum[:, sl]
        gam = jnp.exp(cm)
        gam_prev = jnp.exp(cm - lw[:, sl])
        inv_gam = jnp.exp(-cm)
        kk = k * kk_ref[:, sl]
        kk = kk / jnp.maximum(jnp.sqrt(half_sum(kk * kk)), 1e-12)
        k2 = k * (1.0 + (a - 1.0) * ka_ref[:, sl])
        at = stack(-kk * gam_prev, True).astype(BF16)
        rt = stack(r * gam, True).astype(BF16)
        bt = (kk * a * inv_gam).astype(BF16)
        kt = (k2 * inv_gam).astype(BF16)
        v_st = stack(v, False)
        s0 = s_scr[p]
        lhs = jnp.concatenate([at, rt], axis=0)
        bk = jnp.concatenate([bt, bt, kt, kt], axis=0)
        big = lax.dot_general(lhs, bk, nt, preferred_element_type=F32)
        ls = lax.dot_general(lhs, s0.astype(BF16), nt, preferred_element_type=F32)
        st.append(dict(
            v_st=v_st, v_b=v_st.astype(BF16), bk=bk, s0=s0, gam_last=gam[l - 1:l, :],
            bonus=half_sum(r * k2 * rk_ref[:, sl]) * v,
            m=jnp.where(strict, big[0:2 * l, 0:2 * l], 0.0).astype(BF16),
            a_ak=jnp.where(strict, big[0:2 * l, 2 * l:4 * l], 0.0).astype(BF16),
            a_r=jnp.concatenate([jnp.where(incl, big[2 * l:4 * l, 0:2 * l], 0.0),
                                 jnp.where(incl, big[2 * l:4 * l, 2 * l:4 * l], 0.0)],
                                axis=1).astype(BF16),
            ls_a=ls[0:2 * l], ls_r=ls[2 * l:4 * l]))

    for q in st:
        q["x"] = q["ls_a"] + jnp.dot(q["a_ak"], q["v_b"], preferred_element_type=F32)
    span = 1
    while span < l:
        for q in st:
            q["x"] = q["x"] + jnp.dot(q["m"], q["x"].astype(BF16), preferred_element_type=F32)
        span *= 2
        if span < l:
            for q in st:
                q["m"] = jnp.dot(q["m"], q["m"], preferred_element_type=F32).astype(BF16)

    for p in pairs:
        q, sl = st[p], sls[p]
        x_sol = q["x"]
        y_st = q["ls_r"] + jnp.dot(q["a_r"], jnp.concatenate([x_sol.astype(BF16), q["v_b"]], axis=0),
                                   preferred_element_type=F32)
        uv = jnp.concatenate([jnp.where(stack_mask, x_sol, 0.0),
                              jnp.where(stack_mask, q["v_st"], 0.0)], axis=0).astype(BF16)
        upd = lax.dot_general(uv, q["bk"], tn, preferred_element_type=F32)
        s_scr[p] = (q["s0"] + jnp.where(bd, upd, 0.0)) * q["gam_last"]

        y = jnp.where(lo1, y_st[0:l], y_st[l:2 * l])
        mean = half_sum(y) * (1.0 / hd)
        yc = y - mean
        var = half_sum(yc * yc) * (1.0 / hd)
        y = yc * lax.rsqrt(var + RW_GN_EPS) * lnw_ref[:, sl] + lnb_ref[:, sl]
        y_ref[:, sl] = ((y + q["bonus"]) * g_ref[:, sl].astype(F32)).astype(y_ref.dtype)


def _rwkv_core(r, k, v, lw, a, g, k_k, k_a, r_k, ln_w, ln_b, hd, chunk=64):
    b, s, d = r.shape
    rowv = lambda t: t.reshape(1, d).astype(F32)
    tok = pl.BlockSpec((None, chunk, d), lambda bi, i: (bi, i, 0))
    full = pl.BlockSpec((1, d), lambda bi, i: (0, 0))
    return pl.pallas_call(
        functools.partial(_rwkv_kernel, hd=hd),
        grid=(b, s // chunk),
        in_specs=[tok] * 6 + [full] * 5,
        out_specs=tok,
        out_shape=jax.ShapeDtypeStruct((b, s, d), BF16),
        scratch_shapes=[pltpu.VMEM((d // (2 * hd), 2 * hd, 2 * hd), F32)],
        compiler_params=_cparams("parallel", "arbitrary"),
        name="rwkv_core",
    )(r, k, v, lw, a, g, rowv(k_k), rowv(k_a), rowv(r_k), rowv(ln_w), rowv(ln_b))


def _rwkv_layer(x, nw, sc, sh, g_mod, w_in, mu, w0, w_lora_b, a0, a_lora_b, g_lora_b, k_k, k_a, r_k,
                ln_w, ln_b, w_out, hd):
    r, k, v, lw, a, g = _rwkv_inproj(x, nw, sc, sh, w_in, mu, w0, w_lora_b, a0, a_lora_b, g_lora_b)
    y = _rwkv_core(r, k, v, lw, a, g, k_k, k_a, r_k, ln_w, ln_b, hd)
    return _outproj_residual(x, y, g_mod, w_out.astype(BF16))


MOE_TILE = 1024
MOE_ROWS = 128
MOE_SPECIALISED_ROWS = (192, 256, 320, 384)
MOE_GROUP = 2


def _router_kernel(x_ref, nw_ref, sc_ref, sh_ref, wr_ref, hb_ref, comb_ref, pos_ref, cnt_ref, *, ne):
    t = x_ref.shape[0]
    h = _norm_mod(x_ref[...], nw_ref[...], sc_ref[...], sh_ref[...])
    hb_ref[...] = h.astype(hb_ref.dtype)
    lane = lax.broadcasted_iota(jnp.int32, (t, LANES), 1)
    logits = jnp.where(lane < ne, _dot3(h, wr_ref[...]), NEG_BIG)
    m1 = jnp.max(logits, axis=-1, keepdims=True)
    i1 = jnp.min(jnp.where(logits == m1, lane, LANES), axis=-1, keepdims=True)
    sel1 = lane == i1
    rest = jnp.where(sel1, NEG_BIG, logits)
    m2 = jnp.max(rest, axis=-1, keepdims=True)
    i2 = jnp.min(jnp.where(rest == m2, lane, LANES), axis=-1, keepdims=True)
    sel2 = lane == i2
    e2 = jnp.exp(m2 - m1)
    w1 = 1.0 / (1.0 + e2)
    comb = jnp.where(sel1, w1, 0.0) + jnp.where(sel2, e2 * w1, 0.0)
    sel = jnp.where(sel1 | sel2, 1.0, 0.0)
    below = jnp.where(_tri(t, "strict_lower"), 1.0, 0.0).astype(BF16)
    slot = jnp.dot(below, sel.astype(BF16), preferred_element_type=F32)
    comb_ref[...] = comb.T
    pos_ref[...] = jnp.where(sel > 0.0, slot, -1.0).T
    cnt = jnp.sum(sel, axis=0, keepdims=True).astype(jnp.int32)
    cnt_ref[...] = jnp.broadcast_to(cnt, cnt_ref.shape)


def _moe_router(x, nw, sc, sh, w_router, tile):
    b, s, d = x.shape
    ne = w_router.shape[1]
    nt = s // tile
    vec = pl.BlockSpec((None, 1, d), lambda bi, i: (bi, 0, 0))
    return pl.pallas_call(
        functools.partial(_router_kernel, ne=ne),
        grid=(b, nt),
        in_specs=[pl.BlockSpec((None, tile, d), lambda bi, i: (bi, i, 0)),
                  pl.BlockSpec((1, d), lambda bi, i: (0, 0)), vec, vec,
                  pl.BlockSpec((d, LANES), lambda bi, i: (0, 0))],
        out_specs=[pl.BlockSpec((None, tile, d), lambda bi, i: (bi, i, 0)),
                   pl.BlockSpec((None, LANES, tile), lambda bi, i: (bi, 0, i)),
                   pl.BlockSpec((None, LANES, tile), lambda bi, i: (bi, 0, i)),
                   pl.BlockSpec((None, None, 8, LANES), lambda bi, i: (bi, i, 0, 0))],
        out_shape=[jax.ShapeDtypeStruct((b, s, d), BF16),
                   jax.ShapeDtypeStruct((b, LANES, s), F32),
                   jax.ShapeDtypeStruct((b, LANES, s), F32),
                   jax.ShapeDtypeStruct((b, nt, 8, LANES), jnp.int32)],
        compiler_params=_cparams("parallel", "parallel"),
        name="moe_router",
    )(x, nw.reshape(1, d), sc, sh, _pad_cols(w_router, LANES))


def _expert_kernel(cnt_ref, x_ref, hb_ref, pos_ref, comb_ref, g_ref, wg_ref, wu_ref, wd_ref, *rest,
                   rows, ne, tile, nsub, final_norm):
    fnw_ref = rest[0] if final_norm else None
    o_ref, hg_scr, yacc_scr = rest[-3:]
    bi, gi, e, f = (pl.program_id(a) for a in range(4))
    nf = pl.num_programs(3)
    ntiles = pl.num_programs(1) * nsub

    def run_tile(sub, carry):
        cnt = cnt_ref[(bi * ntiles + gi * nsub + sub) * ne + e]
        nblk = lax.shift_right_logical(cnt + (rows - 1), int(math.log2(rows)))
        tok = pl.ds(pl.multiple_of(sub * tile, tile), tile)

        @pl.when((e == 0) & (f == 0))
        def _():
            o_ref[tok, :] = x_ref[tok, :]

        def run_rows(row0, m):
            rs = pl.ds(row0, m)

            def one_hot():
                slot = pos_ref[sub, pl.ds(e, 1), :]
                want = (lax.broadcasted_iota(jnp.int32, (m, 1), 0) + row0).astype(F32)
                return slot == want

            @pl.when(f == 0)
            def _():
                p = jnp.where(one_hot(), 1.0, 0.0).astype(BF16)
                hg_scr[sub, rs, :] = jnp.dot(p, hb_ref[tok, :],
                                             preferred_element_type=F32).astype(BF16)

            hg = hg_scr[sub, rs, :]
            gate = jnp.dot(hg, wg_ref[...], preferred_element_type=F32)
            up = jnp.dot(hg, wu_ref[...], preferred_element_type=F32)
            act = (gate * _sigmoid(gate) * up).astype(BF16)
            part = jnp.dot(act, wd_ref[...], preferred_element_type=F32)

            @pl.when(f == 0)
            def _():
                yacc_scr[sub, rs, :] = part

            @pl.when(f > 0)
            def _():
                yacc_scr[sub, rs, :] += part

            @pl.when(f == nf - 1)
            def _():
                hit = one_hot()
                wrow = comb_ref[sub, pl.ds(e, 1), :]
                wgt = jnp.sum(jnp.where(hit, wrow, 0.0), axis=-1, keepdims=True)
                yw = (yacc_scr[sub, rs, :] * wgt * g_ref[...]).astype(BF16)
                p = jnp.where(hit, 1.0, 0.0).astype(BF16)
                o_ref[tok, :] += lax.dot_general(p, yw, (((0,), (0,)), ((), ())),
                                                 preferred_element_type=F32)

        lo = 0
        for m in MOE_SPECIALISED_ROWS:
            @pl.when((cnt > lo) & (cnt <= m))
            def _():
                run_rows(0, m)
            lo = m

        @pl.when(cnt > lo)
        def _():
            def body(j, c):
                run_rows(pl.multiple_of(j * rows, rows), rows)
                return c
            lax.fori_loop(0, nblk, body, 0)

        if final_norm:
            @pl.when((e == ne - 1) & (f == nf - 1))
            def _():
                xo = o_ref[tok, :]
                ms = jnp.mean(xo * xo, axis=-1, keepdims=True)
                o_ref[tok, :] = xo * lax.rsqrt(ms + RMS_EPS) * fnw_ref[...]

        return carry

    lax.fori_loop(0, nsub, run_tile, 0)


def _moe_experts(x, hb, pos, comb, counts, g, w_gu_bf16, w_down_bf16, layer, tile, nsub, rows,
                 final_norm_w=None, tf=896):
    b, s, d = x.shape
    final_norm = final_norm_w is not None
    extra_specs = [pl.BlockSpec((1, d), lambda bi, i, e, f, c: (0, 0))] if final_norm else []
    extra_args = [final_norm_w.reshape(1, d).astype(F32)] if final_norm else []
    _, ne, dff, _ = w_down_bf16.shape
    nf = dff // tf
    grp = tile * nsub
    once = pl.Buffered(1)
    grid_spec = pltpu.PrefetchScalarGridSpec(
        num_scalar_prefetch=1,
        grid=(b, s // grp, ne, nf),
        in_specs=[
            pl.BlockSpec((None, grp, d), lambda bi, i, e, f, c: (bi, i, 0), pipeline_mode=once),
            pl.BlockSpec((None, grp, d), lambda bi, i, e, f, c: (bi, i, 0), pipeline_mode=once),
            pl.BlockSpec((None, nsub, 8, tile), lambda bi, i, e, f, c: (bi, i, 0, 0)),
            pl.BlockSpec((None, nsub, 8, tile), lambda bi, i, e, f, c: (bi, i, 0, 0)),
            pl.BlockSpec((None, 1, d), lambda bi, i, e, f, c: (bi, 0, 0)),
            pl.BlockSpec((None, None, d, tf), lambda bi, i, e, f, c: (layer, e, 0, f)),
            pl.BlockSpec((None, None, d, tf), lambda bi, i, e, f, c: (layer, e, 0, f + nf)),
            pl.BlockSpec((None, None, tf, d), lambda bi, i, e, f, c: (layer, e, f, 0)),
        ] + extra_specs,
        out_specs=pl.BlockSpec((None, grp, d), lambda bi, i, e, f, c: (bi, i, 0)),
        scratch_shapes=[pltpu.VMEM((nsub, tile, d), BF16), pltpu.VMEM((nsub, tile, d), F32)],
    )
    return pl.pallas_call(
        functools.partial(_expert_kernel, rows=rows, ne=ne, tile=tile, nsub=nsub,
                          final_norm=final_norm),
        grid_spec=grid_spec,
        out_shape=jax.ShapeDtypeStruct((b, s, d), F32),
        input_output_aliases={1: 0},
        compiler_params=_cparams("parallel", "parallel", "arbitrary", "arbitrary"),
        name="moe_experts",
    )(counts, x, hb, pos, comb, g, w_gu_bf16, w_gu_bf16, w_down_bf16, *extra_args)


def _moe_layer(x, nw, sc, sh, g, w_router, w_gu_bf16, w_down_bf16, layer, final_norm_w=None):
    b, s, _ = x.shape
    ne = w_router.shape[1]
    tile = min(MOE_TILE, s)
    nsub = min(MOE_GROUP, s // tile)
    hb, comb_t, pos_t, counts = _moe_router(x, nw, sc, sh, w_router, tile)
    counts = counts[:, :, 0, :ne].reshape(-1)
    by_tile = lambda a: jnp.swapaxes(a[:, :8].reshape(b, 8, s // tile, tile), 1, 2)
    return _moe_experts(x, hb, by_tile(pos_t), by_tile(comb_t), counts, g, w_gu_bf16, w_down_bf16,
                        layer, tile, nsub, MOE_ROWS, final_norm_w)


def _final_norm_kernel(x_ref, w_ref, o_ref):
    x = x_ref[...]
    ms = jnp.mean(x * x, axis=-1, keepdims=True)
    o_ref[...] = x * lax.rsqrt(ms + RMS_EPS) * w_ref[...]


def _final_norm(x, w, tm=1024):
    b, s, d = x.shape
    return pl.pallas_call(
        _final_norm_kernel,
        grid=(b, s // tm),
        in_specs=[pl.BlockSpec((None, tm, d), lambda bi, i: (bi, i, 0)),
                  pl.BlockSpec((1, d), lambda bi, i: (0, 0))],
        out_specs=pl.BlockSpec((None, tm, d), lambda bi, i: (bi, i, 0)),
        out_shape=jax.ShapeDtypeStruct((b, s, d), F32),
        compiler_params=_cparams("parallel", "parallel"),
        name="final_norm",
    )(x, w.reshape(1, d))


ML_HEADS = 4
SSD_GROUPS, SSD_HEADDIM, SSD_STATE = 4, 64, 128
RW_HEADDIM = 64


def kernel(x, c, ada_w, ada_b, norm_w, final_norm_w, ml_w_in, ml_gate_b, ml_norm_w, ml_w_out, ssd_w_in, ssd_conv_w, ssd_conv_b, ssd_dt_bias, ssd_a_log, ssd_d, ssd_norm_w, ssd_w_out, rw_w_in, rw_mu, rw_w0, rw_w_lora_b, rw_a0, rw_a_lora_b, rw_g_lora_b, rw_k_k, rw_k_a, rw_r_k, rw_ln_w, rw_ln_b, rw_w_out, lru_w_in, lru_conv_w, lru_conv_b, lru_gx_w, lru_gx_b, lru_ga_w, lru_ga_b, lru_lambda, lru_w_out, ffn_w_gu, ffn_w_down, moe_router, moe_w_gu, moe_w_down):
    depth = ada_w.shape[0]
    mod = _ada_modulation(c, ada_w, ada_b)
    moe_gu_b, moe_down_b = moe_w_gu.astype(BF16), moe_w_down.astype(BF16)
    for layer in range(depth):
        sh_t, sc_t, g_t, sh_c, sc_c, g_c = [m[:, None, :] for m in jnp.split(mod[layer], 6, axis=-1)]
        kind, j = layer % 4, layer // 4
        nw_t, nw_c = norm_w[layer, 0], norm_w[layer, 1]
        if kind == 0:
            x = _mlstm_layer(x, nw_t, sc_t, sh_t, g_t, ml_w_in[j], ml_gate_b[j], ml_norm_w[j],
                             ml_w_out[j], ML_HEADS)
        elif kind == 1:
            x = _ssd_layer(x, nw_t, sc_t, sh_t, g_t, ssd_w_in[j], ssd_conv_w[j], ssd_conv_b[j],
                           ssd_dt_bias[j], ssd_a_log[j], ssd_d[j], ssd_norm_w[j], ssd_w_out[j],
                           SSD_GROUPS, SSD_HEADDIM, SSD_STATE)
        elif kind == 2:
            x = _rwkv_layer(x, nw_t, sc_t, sh_t, g_t, rw_w_in[j], rw_mu[j], rw_w0[j], rw_w_lora_b[j],
                            rw_a0[j], rw_a_lora_b[j], rw_g_lora_b[j], rw_k_k[j], rw_k_a[j], rw_r_k[j],
                            rw_ln_w[j], rw_ln_b[j], rw_w_out[j], RW_HEADDIM)
        else:
            x = _rglru_layer(x, nw_t, sc_t, sh_t, g_t, lru_w_in[j], lru_conv_w[j], lru_conv_b[j],
                             lru_gx_w[j], lru_gx_b[j], lru_ga_w[j], lru_ga_b[j], lru_lambda[j],
                             lru_w_out[j])
        if layer % 2 == 0:
            x = _ffn_dense(x, nw_c, sc_c, sh_c, g_c, ffn_w_gu[layer // 2].astype(BF16),
                           ffn_w_down[layer // 2].astype(BF16))
        else:
            closing = final_norm_w if layer == depth - 1 else None
            x = _moe_layer(x, nw_c, sc_c, sh_c, g_c, moe_router[layer // 2], moe_gu_b, moe_down_b,
                           layer // 2, closing)
    if depth % 2 == 0:
        return x
    return _final_norm(x, final_norm_w)
```

```python
import functools
import math

import jax
import jax.numpy as jnp
from jax import lax
from jax.experimental import pallas as pl
from jax.experimental.pallas import tpu as pltpu

F32 = jnp.float32
BF16 = jnp.bfloat16

RMS_EPS = 1e-6
LANES = 128
VMEM_LIMIT_BYTES = 56 * 1024 * 1024


def _cparams(*sem):
    return pltpu.CompilerParams(dimension_semantics=sem, vmem_limit_bytes=VMEM_LIMIT_BYTES)


def _sigmoid(x):
    return 1.0 / (1.0 + jnp.exp(-x))


def _softplus(x):
    return jnp.maximum(x, 0.0) + jnp.log(1.0 + jnp.exp(-jnp.abs(x)))


def _bdot(a, b):
    return jnp.dot(a.astype(BF16), b.astype(BF16), preferred_element_type=F32)


def _split_bf16(a):
    hi = a.astype(BF16)
    lo = (a - hi.astype(F32)).astype(BF16)
    return hi, lo


def _dot3(a, b):
    a_hi, a_lo = _split_bf16(a)
    b_hi, b_lo = _split_bf16(b)
    d = functools.partial(jnp.dot, preferred_element_type=F32)
    return d(a_hi, b_hi) + d(a_lo, b_hi) + d(a_hi, b_lo)


def _norm_mod(x, nw, sc, sh):
    ms = jnp.mean(x * x, axis=-1, keepdims=True)
    y = x * lax.rsqrt(ms + RMS_EPS) * nw
    return y * (1.0 + sc) + sh


def _pad_cols(w, n):
    return jnp.pad(w, ((0, 0), (0, n - w.shape[1])))


def _ada_kernel(c_ref, w_ref, b_ref, o_ref):
    c = c_ref[...]
    cond = c * _sigmoid(c)
    o_ref[...] = _dot3(cond, w_ref[...]) + b_ref[...]


def _ada_modulation(c, ada_w, ada_b):
    depth, d, n = ada_w.shape
    b = c.shape[0]
    rows = 8
    cp = jnp.pad(c, ((0, rows - b), (0, 0)))
    tn = 1024
    out = pl.pallas_call(
        _ada_kernel,
        grid=(depth, n // tn),
        in_specs=[
            pl.BlockSpec((rows, d), lambda l, j: (0, 0)),
            pl.BlockSpec((None, d, tn), lambda l, j: (l, 0, j)),
            pl.BlockSpec((None, 1, tn), lambda l, j: (l, 0, j)),
        ],
        out_specs=pl.BlockSpec((None, rows, tn), lambda l, j: (l, 0, j)),
        out_shape=jax.ShapeDtypeStruct((depth, rows, n), F32),
        compiler_params=_cparams("parallel", "parallel"),
        name="ada_modulation",
    )(cp, ada_w, ada_b.reshape(depth, 1, n))
    return out[:, :b]


def _nmm_kernel(x_ref, nw_ref, sc_ref, sh_ref, w_ref, *o_refs, segs):
    hb = _norm_mod(x_ref[...], nw_ref[...], sc_ref[...], sh_ref[...]).astype(BF16)
    off = 0
    for o_ref, n in zip(o_refs, segs):
        o_ref[...] = jnp.dot(hb, w_ref[:, off:off + n],
                             preferred_element_type=F32).astype(o_ref.dtype)
        off += n


def _norm_mod_matmul(x, nw, sc, sh, w_bf16, segs, dtypes, tm=512):
    b, s, d = x.shape
    ntot = w_bf16.shape[1]
    assert sum(segs) == ntot and s % tm == 0
    vec = pl.BlockSpec((None, 1, d), lambda bi, i: (bi, 0, 0))
    return pl.pallas_call(
        functools.partial(_nmm_kernel, segs=tuple(segs)),
        grid=(b, s // tm),
        in_specs=[
            pl.BlockSpec((None, tm, d), lambda bi, i: (bi, i, 0)),
            pl.BlockSpec((1, d), lambda bi, i: (0, 0)),
            vec, vec,
            pl.BlockSpec((d, ntot), lambda bi, i: (0, 0)),
        ],
        out_specs=[pl.BlockSpec((None, tm, n), lambda bi, i: (bi, i, 0)) for n in segs],
        out_shape=[jax.ShapeDtypeStruct((b, s, n), dt) for n, dt in zip(segs, dtypes)],
        compiler_params=_cparams("parallel", "parallel"),
        name="norm_mod_matmul",
    )(x, nw.reshape(1, d), sc, sh, w_bf16)


def _outproj_kernel(x_ref, y_ref, g_ref, w_ref, o_ref):
    o_ref[...] = x_ref[...] + g_ref[...] * jnp.dot(
        y_ref[...], w_ref[...], preferred_element_type=F32)


def _outproj_residual(x, y, g, w_bf16, tm=512):
    b, s, d = x.shape
    k = y.shape[-1]
    return pl.pallas_call(
        _outproj_kernel,
        grid=(b, s // tm),
        in_specs=[
            pl.BlockSpec((None, tm, d), lambda bi, i: (bi, i, 0)),
            pl.BlockSpec((None, tm, k), lambda bi, i: (bi, i, 0)),
            pl.BlockSpec((None, 1, d), lambda bi, i: (bi, 0, 0)),
            pl.BlockSpec((k, d), lambda bi, i: (0, 0)),
        ],
        out_specs=pl.BlockSpec((None, tm, d), lambda bi, i: (bi, i, 0)),
        out_shape=jax.ShapeDtypeStruct((b, s, d), F32),
        compiler_params=_cparams("parallel", "parallel"),
        name="outproj_residual",
    )(x, y, g, w_bf16)


def _ffn_kernel(x_ref, nw_ref, sc_ref, sh_ref, g_ref, wg_ref, wu_ref, wd_ref, o_ref,
                h_scr, acc_scr):
    f = pl.program_id(2)

    @pl.when(f == 0)
    def _():
        h_scr[...] = _norm_mod(x_ref[...], nw_ref[...], sc_ref[...], sh_ref[...]).astype(BF16)
        acc_scr[...] = jnp.zeros_like(acc_scr)

    hb = h_scr[...]
    gate = jnp.dot(hb, wg_ref[...], preferred_element_type=F32)
    up = jnp.dot(hb, wu_ref[...], preferred_element_type=F32)
    act = (gate * _sigmoid(gate) * up).astype(BF16)
    acc_scr[...] += jnp.dot(act, wd_ref[...], preferred_element_type=F32)

    @pl.when(f == pl.num_programs(2) - 1)
    def _():
        o_ref[...] = x_ref[...] + g_ref[...] * acc_scr[...]


def _ffn_dense(x, nw, sc, sh, g, w_gu_bf16, w_down_bf16, tm=1024, tf=512):
    b, s, d = x.shape
    dff = w_down_bf16.shape[0]
    nf = dff // tf
    vec = pl.BlockSpec((None, 1, d), lambda bi, i, f: (bi, 0, 0))
    return pl.pallas_call(
        _ffn_kernel,
        grid=(b, s // tm, nf),
        in_specs=[
            pl.BlockSpec((None, tm, d), lambda bi, i, f: (bi, i, 0)),
            pl.BlockSpec((1, d), lambda bi, i, f: (0, 0)),
            vec, vec, vec,
            pl.BlockSpec((d, tf), lambda bi, i, f: (0, f)),
            pl.BlockSpec((d, tf), lambda bi, i, f: (0, f + nf)),
            pl.BlockSpec((tf, d), lambda bi, i, f: (f, 0)),
        ],
        out_specs=pl.BlockSpec((None, tm, d), lambda bi, i, f: (bi, i, 0)),
        out_shape=jax.ShapeDtypeStruct((b, s, d), F32),
        scratch_shapes=[pltpu.VMEM((tm, d), BF16), pltpu.VMEM((tm, d), F32)],
        input_output_aliases={0: 0},
        compiler_params=_cparams("parallel", "parallel", "arbitrary"),
        name="ffn_dense",
    )(x, nw.reshape(1, d), sc, sh, g, w_gu_bf16, w_gu_bf16, w_down_bf16)


CONV_W = 4
TAIL = 8


def _causal_conv(pad_ref, x, w_ref, b_ref, first):
    l = x.shape[0]

    @pl.when(first)
    def _():
        pad_ref[0:TAIL, :] = jnp.zeros((TAIL, x.shape[1]), F32)

    pad_ref[TAIL:2 * TAIL, :] = x[0:TAIL, :]
    y = b_ref[...] + w_ref[CONV_W - 1:CONV_W, :] * x
    head = y[0:TAIL, :]
    for k in range(CONV_W - 1):
        shift = CONV_W - 1 - k
        y = y + w_ref[k:k + 1, :] * pltpu.roll(x, shift, axis=0)
        head = head + w_ref[k:k + 1, :] * pad_ref[TAIL - shift:2 * TAIL - shift, :]
    pad_ref[0:TAIL, :] = x[l - TAIL:, :]
    return jnp.concatenate([head, y[TAIL:, :]], axis=0)


LRU_C = 8.0


def _gelu_tanh(x):
    return 0.5 * x * (1.0 + jnp.tanh(math.sqrt(2.0 / math.pi) * (x + 0.044715 * (x * x * x))))


def _lru_kernel(xbr_ref, ybr_ref, cw_ref, cb_ref, gxw_ref, gxb_ref, gaw_ref, gab_ref, lam_ref,
                o_ref, pad_scr, h_scr):
    i = pl.program_id(1)
    l, w = xbr_ref.shape
    nblk, blk = gxw_ref.shape[0], gxw_ref.shape[1]

    @pl.when(i == 0)
    def _():
        h_scr[...] = jnp.zeros_like(h_scr)

    xb = _causal_conv(pad_scr, xbr_ref[...], cw_ref, cb_ref, i == 0)
    xbb = xb.astype(BF16)
    gx = jnp.concatenate([jnp.dot(xbb[:, n * blk:(n + 1) * blk], gxw_ref[n],
                                  preferred_element_type=F32) for n in range(nblk)], axis=-1)
    ga = jnp.concatenate([jnp.dot(xbb[:, n * blk:(n + 1) * blk], gaw_ref[n],
                                  preferred_element_type=F32) for n in range(nblk)], axis=-1)
    gate_x = _sigmoid(gx + gxb_ref[...])
    gate_a = _sigmoid(ga + gab_ref[...])
    log_a = LRU_C * gate_a * (-_softplus(-lam_ref[...]))
    a = jnp.exp(log_a)
    u = jnp.sqrt(1.0 - jnp.exp(2.0 * log_a)) * gate_x * xb

    row = lax.broadcasted_iota(jnp.int32, (l, 1), 0)
    d = 1
    while d < l:
        if d < TAIL:
            a_sh = pltpu.roll(a, d, axis=0)
            u_sh = pltpu.roll(u, d, axis=0)
            valid = row >= d
            u = jnp.where(valid, a * u_sh + u, u)
            a = jnp.where(valid, a * a_sh, a)
        else:
            u = jnp.concatenate([u[:d], a[d:] * u[:l - d] + u[d:]], axis=0)
            a = jnp.concatenate([a[:d], a[d:] * a[:l - d]], axis=0)
        d *= 2
    hs = a * h_scr[0:1, :] + u
    h_scr[...] = jnp.broadcast_to(hs[l - 1:l, :], h_scr.shape)
    o_ref[...] = (_gelu_tanh(ybr_ref[...].astype(F32)) * hs).astype(o_ref.dtype)


def _rglru_core(x_br, y_br, conv_w, conv_b, gx_w, gx_b, ga_w, ga_b, lam, chunk=256):
    b, s, w = x_br.shape
    nblk, blk, _ = gx_w.shape
    row = lambda a: a.reshape(1, w).astype(F32)
    full2 = lambda shape: pl.BlockSpec(shape, lambda bi, i: (0, 0))
    full3 = lambda shape: pl.BlockSpec(shape, lambda bi, i: (0, 0, 0))
    return pl.pallas_call(
        _lru_kernel,
        grid=(b, s // chunk),
        in_specs=[
            pl.BlockSpec((None, chunk, w), lambda bi, i: (bi, i, 0)),
            pl.BlockSpec((None, chunk, w), lambda bi, i: (bi, i, 0)),
            full2((CONV_W, w)), full2((1, w)),
            full3((nblk, blk, blk)), full2((1, w)),
            full3((nblk, blk, blk)), full2((1, w)),
            full2((1, w)),
        ],
        out_specs=pl.BlockSpec((None, chunk, w), lambda bi, i: (bi, i, 0)),
        out_shape=jax.ShapeDtypeStruct((b, s, w), BF16),
        scratch_shapes=[pltpu.VMEM((2 * TAIL, w), F32), pltpu.VMEM((8, w), F32)],
        compiler_params=_cparams("parallel", "arbitrary"),
        name="rglru_core",
    )(x_br, y_br, conv_w, row(conv_b), gx_w.astype(BF16), row(gx_b), ga_w.astype(BF16), row(ga_b),
      row(lam))


def _rglru_layer(x, nw, sc, sh, g, w_in, conv_w, conv_b, gx_w, gx_b, ga_w, ga_b, lam, w_out):
    w = w_in.shape[1] // 2
    y_br, x_br = _norm_mod_matmul(x, nw, sc, sh, w_in.astype(BF16), [w, w], [BF16, F32])
    out = _rglru_core(x_br, y_br, conv_w, conv_b, gx_w, gx_b, ga_w, ga_b, lam)
    return _outproj_residual(x, out, g, w_out.astype(BF16))


NEG_BIG = -1e30


def _split3(a):
    hi = a.astype(BF16)
    r = a - hi.astype(F32)
    mid = r.astype(BF16)
    lo = (r - mid.astype(F32)).astype(BF16)
    return hi, mid, lo


def _tri(l, kind):
    r = lax.broadcasted_iota(jnp.int32, (l, l), 0)
    c = lax.broadcasted_iota(jnp.int32, (l, l), 1)
    m = {"lower": r >= c, "strict_lower": r > c, "upper": r <= c}[kind]
    return m


def _cumsum_rows(a):
    t = jnp.where(_tri(a.shape[0], "lower"), 1.0, 0.0).astype(BF16)
    return sum(jnp.dot(t, p, preferred_element_type=F32) for p in _split3(a))


def _cumsum_lanes(a):
    t = jnp.where(_tri(a.shape[1], "upper"), 1.0, 0.0).astype(BF16)
    return sum(jnp.dot(p, t, preferred_element_type=F32) for p in _split3(a))


def _mlstm_kernel(q_ref, k_ref, v_ref, o_ref, gc_ref, gr_ref, gbr_ref, gbc_ref, nw_ref, y_ref,
                  c_scr, n_scr, m_scr, *, heads, dk, dv):
    i = pl.program_id(1)
    l = q_ref.shape[0]

    @pl.when(i == 0)
    def _():
        c_scr[...] = jnp.zeros_like(c_scr)
        n_scr[...] = jnp.zeros_like(n_scr)
        m_scr[...] = jnp.zeros_like(m_scr)

    gc = gc_ref[...] + gbr_ref[...]
    gr = gr_ref[...] + gbc_ref[...]
    li_col = gc[:, 0:heads]
    li_row = gr[0:heads, :]
    g_col = _cumsum_rows(-_softplus(-gc))[:, heads:2 * heads]
    g_row = _cumsum_lanes(-_softplus(-gr))[heads:2 * heads, :]
    causal = _tri(l, "lower")
    scale = dk ** -0.5

    st = []
    for h in range(heads):
        a_col = g_col[:, h:h + 1]
        b_row = li_row[h:h + 1, :] - g_row[h:h + 1, :]
        m_prev = m_scr[h, 0:1, 0:1]
        logd = jnp.where(causal, a_col + b_row, NEG_BIG)
        log_inter = a_col + m_prev
        m_row = jnp.maximum(jnp.max(logd, axis=-1, keepdims=True), log_inter)
        qs = (q_ref[:, h * dk:(h + 1) * dk].astype(F32) * scale).astype(BF16)
        k = k_ref[:, h * dk:(h + 1) * dk]
        c_mat = c_scr[h]
        st.append(dict(
            a_col=a_col, b_col=li_col[:, h:h + 1] - a_col, m_prev=m_prev, m_row=m_row, qs=qs, k=k,
            c_mat=c_mat, e_inter=jnp.exp(log_inter - m_row),
            s=lax.dot_general(qs, k, (((1,), (1,)), ((), ())), preferred_element_type=F32)
            * jnp.exp(logd - m_row),
            qc=jnp.dot(qs, c_mat.astype(BF16), preferred_element_type=F32)))

    for h, t in enumerate(st):
        v = v_ref[:, h * dv:(h + 1) * dv]
        n_vec = n_scr[h, 0:1, :]
        num = jnp.dot(t["s"].astype(BF16), v, preferred_element_type=F32) + t["e_inter"] * t["qc"]
        den = (jnp.sum(t["s"], axis=-1, keepdims=True)
               + t["e_inter"] * jnp.sum(t["qs"].astype(F32) * n_vec, axis=-1, keepdims=True))
        denom = jnp.maximum(jnp.abs(den), jnp.exp(-t["m_row"]))
        hh = num / denom
        hh = hh * lax.rsqrt(jnp.mean(hh * hh, axis=-1, keepdims=True) + RMS_EPS)
        gate = _sigmoid(o_ref[:, h * dv:(h + 1) * dv].astype(F32))
        y_ref[:, h * dv:(h + 1) * dv] = (hh * nw_ref[:, h * dv:(h + 1) * dv] * gate).astype(y_ref.dtype)

    for h, t in enumerate(st):
        v = v_ref[:, h * dv:(h + 1) * dv]
        n_vec = n_scr[h, 0:1, :]
        g_last = t["a_col"][l - 1:l, :]
        log_w = g_last + t["b_col"]
        m_new = jnp.maximum(g_last + t["m_prev"], jnp.max(log_w, axis=0, keepdims=True))
        kw = t["k"].astype(F32) * jnp.exp(log_w - m_new)
        decay = jnp.exp(g_last + t["m_prev"] - m_new)
        c_scr[h] = decay * t["c_mat"] + lax.dot_general(
            kw.astype(BF16), v, (((0,), (0,)), ((), ())), preferred_element_type=F32)
        n_scr[h, 0:1, :] = decay * n_vec + jnp.sum(kw, axis=0, keepdims=True)
        m_scr[h] = jnp.broadcast_to(m_new, m_scr.shape[1:])


def _mlstm_core(q, k, v, o, gates, gate_b, norm_w, heads, chunk=256):
    b, s, hk = q.shape
    hv = v.shape[-1]
    dk, dv = hk // heads, hv // heads
    g2 = 2 * heads
    gates_t = jnp.swapaxes(gates, 1, 2)
    gb = gate_b.reshape(1, g2).astype(F32)
    tok = lambda n: pl.BlockSpec((None, chunk, n), lambda bi, i: (bi, i, 0))
    full = lambda shape: pl.BlockSpec(shape, lambda bi, i: (0, 0))
    return pl.pallas_call(
        functools.partial(_mlstm_kernel, heads=heads, dk=dk, dv=dv),
        grid=(b, s // chunk),
        in_specs=[tok(hk), tok(hk), tok(hv), tok(hv), tok(g2),
                  pl.BlockSpec((None, g2, chunk), lambda bi, i: (bi, 0, i)),
                  full((1, g2)), full((g2, 1)), full((1, hv))],
        out_specs=tok(hv),
        out_shape=jax.ShapeDtypeStruct((b, s, hv), BF16),
        scratch_shapes=[pltpu.VMEM((heads, dk, dv), F32), pltpu.VMEM((heads, 8, dk), F32),
                        pltpu.VMEM((heads, 8, LANES), F32)],
        compiler_params=_cparams("parallel", "arbitrary"),
        name="mlstm_core",
    )(q, k, v, o, gates, gates_t, gb, gb.reshape(g2, 1), norm_w.reshape(1, hv).astype(F32))


def _mlstm_layer(x, nw, sc, sh, g, w_in, gate_b, norm_w, w_out, heads):
    hv = w_out.shape[0]
    hk = (w_in.shape[1] - 2 * hv - 2 * heads) // 2
    w_pad = _pad_cols(w_in, 2 * hk + 2 * hv + LANES).astype(BF16)
    q, k, v, o, gates = _norm_mod_matmul(x, nw, sc, sh, w_pad, [hk, hk, hv, hv, LANES],
                                         [BF16, BF16, BF16, BF16, F32])
    y = _mlstm_core(q, k, v, o, gates[..., :2 * heads], gate_b, norm_w, heads)
    return _outproj_residual(x, y, g, w_out.astype(BF16))


def _pair_select(lo_half, a, b):
    return jnp.where(lo_half, a, b)


def _ssd_kernel(z_ref, xbc_ref, dtc_ref, dtr_ref, cw_ref, cb_ref, dbr_ref, dbc_ref, alr_ref, alc_ref,
                dsk_ref, nw_ref, y_ref, pad_scr, st_scr, *, groups, hpg, hd, ns):
    i = pl.program_id(1)
    l = z_ref.shape[0]
    inner = groups * hpg * hd
    gw = hpg * hd
    pair = 2 * hd

    @pl.when(i == 0)
    def _():
        st_scr[...] = jnp.zeros_like(st_scr)

    conv = _causal_conv(pad_scr, xbc_ref[...].astype(F32), cw_ref, cb_ref, i == 0)
    conv = conv * _sigmoid(conv)
    xs = conv[:, 0:inner]
    xs_b = xs.astype(BF16)
    bm = conv[:, inner:inner + groups * ns].astype(BF16)
    cm = conv[:, inner + groups * ns:inner + 2 * groups * ns].astype(BF16)

    dt_col = _softplus(dtc_ref[...] + dbr_ref[...])
    dt_row = _softplus(dtr_ref[...] + dbc_ref[...])
    acum_col = _cumsum_rows(dt_col * (-jnp.exp(alr_ref[...])))
    acum_row = _cumsum_lanes(dt_row * (-jnp.exp(alc_ref[...])))
    a_last = acum_col[l - 1:l, :]
    ea_col = jnp.exp(acum_col)
    ws_col = jnp.exp(a_last - acum_col) * dt_col
    ea_last = jnp.exp(a_last)

    causal = _tri(l, "lower")
    lo_half = lax.broadcasted_iota(jnp.int32, (1, pair), 1) < hd

    for g in range(groups):
        bg = bm[:, g * ns:(g + 1) * ns]
        cg = cm[:, g * ns:(g + 1) * ns]
        cb = lax.dot_general(cg, bg, (((1,), (1,)), ((), ())), preferred_element_type=F32)
        st = st_scr[g]
        y_inter = jnp.dot(cg, st.astype(BF16), preferred_element_type=F32)
        xw_parts, y_parts, dec_parts = [], [], []
        for p in range(hpg // 2):
            h0 = g * hpg + 2 * p
            c0 = g * gw + p * pair
            xp = xs_b[:, c0:c0 + pair]
            ys = []
            for h in (h0, h0 + 1):
                dec = jnp.exp(jnp.where(causal, acum_col[:, h:h + 1] - acum_row[h:h + 1, :], NEG_BIG))
                wts = (cb * dec * dt_row[h:h + 1, :]).astype(BF16)
                ys.append(jnp.dot(wts, xp, preferred_element_type=F32))
            sel = lambda a: _pair_select(lo_half, a[:, h0:h0 + 1], a[:, h0 + 1:h0 + 2])
            y_parts.append(_pair_select(lo_half, ys[0], ys[1])
                           + sel(ea_col) * y_inter[:, p * pair:(p + 1) * pair])
            xw_parts.append((xs[:, c0:c0 + pair] * sel(ws_col)).astype(BF16))
            dec_parts.append(sel(ea_last))
        xw = jnp.concatenate(xw_parts, axis=-1)
        st_scr[g] = jnp.concatenate(dec_parts, axis=-1) * st + lax.dot_general(
            bg, xw, (((0,), (0,)), ((), ())), preferred_element_type=F32)
        yg = jnp.concatenate(y_parts, axis=-1)
        sl = slice(g * gw, (g + 1) * gw)
        yg = yg + dsk_ref[:, sl] * xs[:, sl]
        zg = z_ref[:, sl].astype(F32)
        yg = yg * (zg * _sigmoid(zg))
        yg = yg * lax.rsqrt(jnp.mean(yg * yg, axis=-1, keepdims=True) + RMS_EPS)
        y_ref[:, sl] = (yg * nw_ref[:, sl]).astype(y_ref.dtype)


def _ssd_core(z, xbc, dt, conv_w, conv_b, dt_bias, a_log, d_skip, norm_w, groups, hd, ns, chunk=128):
    b, s, inner = z.shape
    heads = dt.shape[-1]
    hpg = heads // groups
    cdim = xbc.shape[-1]
    dt_t = jnp.swapaxes(dt, 1, 2)
    rowv = lambda a: a.reshape(1, -1).astype(F32)
    colv = lambda a: a.reshape(-1, 1).astype(F32)
    tok = lambda n: pl.BlockSpec((None, chunk, n), lambda bi, i: (bi, i, 0))
    full = lambda shape: pl.BlockSpec(shape, lambda bi, i: (0, 0))
    return pl.pallas_call(
        functools.partial(_ssd_kernel, groups=groups, hpg=hpg, hd=hd, ns=ns),
        grid=(b, s // chunk),
        in_specs=[tok(inner), tok(cdim), tok(heads),
                  pl.BlockSpec((None, heads, chunk), lambda bi, i: (bi, 0, i)),
                  full((CONV_W, cdim)), full((1, cdim)),
                  full((1, heads)), full((heads, 1)), full((1, heads)), full((heads, 1)),
                  full((1, inner)), full((1, inner))],
        out_specs=tok(inner),
        out_shape=jax.ShapeDtypeStruct((b, s, inner), BF16),
        scratch_shapes=[pltpu.VMEM((2 * TAIL, cdim), F32),
                        pltpu.VMEM((groups, ns, hpg * hd), F32)],
        compiler_params=_cparams("parallel", "arbitrary"),
        name="ssd_core",
    )(z, xbc, dt, dt_t, conv_w, rowv(conv_b), rowv(dt_bias), colv(dt_bias), rowv(a_log), colv(a_log),
      rowv(jnp.repeat(d_skip, hd)), rowv(norm_w))


def _ssd_layer(x, nw, sc, sh, g, w_in, conv_w, conv_b, dt_bias, a_log, d_skip, norm_w, w_out,
               groups, hd, ns):
    inner = w_out.shape[0]
    heads = inner // hd
    cdim = inner + 2 * groups * ns
    w_pad = _pad_cols(w_in, inner + cdim + LANES).astype(BF16)
    z, xbc, dt = _norm_mod_matmul(x, nw, sc, sh, w_pad, [inner, cdim, LANES], [BF16, BF16, F32])
    y = _ssd_core(z, xbc, dt[..., :heads], conv_w, conv_b, dt_bias, a_log, d_skip, norm_w,
                  groups, hd, ns)
    return _outproj_residual(x, y, g, w_out.astype(BF16))


def _rwkv_inproj_kernel(x_ref, nw_ref, sc_ref, sh_ref, mu_ref, wr_ref, wk_ref, wv_ref,
                        wdw_ref, wda_ref, wdg_ref, w0_ref, wlb_ref, a0_ref, alb_ref, glb_ref,
                        r_ref, k_ref, v_ref, lw_ref, a_ref, g_ref, last_scr):
    i = pl.program_id(1)
    tm = x_ref.shape[0]

    @pl.when(i == 0)
    def _():
        last_scr[...] = jnp.zeros_like(last_scr)

    h = _norm_mod(x_ref[...], nw_ref[...], sc_ref[...], sh_ref[...])
    row = lax.broadcasted_iota(jnp.int32, (tm, 1), 0)
    h_prev = jnp.where(row == 0, last_scr[0:1, :], pltpu.roll(h, 1, axis=0))
    last_scr[...] = jnp.broadcast_to(h[tm - 1:tm, :], last_scr.shape)
    xx = h_prev - h

    def proj(b, w_ref):
        xb = (h + xx * mu_ref[b:b + 1, :]).astype(BF16)
        return jnp.dot(xb, w_ref[...], preferred_element_type=F32)

    r_ref[...] = proj(0, wr_ref).astype(r_ref.dtype)
    k_ref[...] = proj(1, wk_ref).astype(k_ref.dtype)
    v_ref[...] = proj(2, wv_ref).astype(v_ref.dtype)
    dw = proj(3, wdw_ref)
    da = proj(4, wda_ref)
    dg = proj(5, wdg_ref)
    z = w0_ref[...] + _bdot(jnp.tanh(dw), wlb_ref[...])
    w_log = -_softplus(-z) - 0.5
    lw_ref[...] = -jnp.exp(w_log)
    a_ref[...] = _sigmoid(a0_ref[...] + _bdot(da, alb_ref[...])).astype(a_ref.dtype)
    g_ref[...] = _bdot(_sigmoid(dg), glb_ref[...]).astype(g_ref.dtype)


def _pad_rows(w, n):
    return jnp.pad(w, ((0, n - w.shape[0]), (0, 0)))


def _round_up(n, m):
    return -(-n // m) * m


def _rwkv_inproj(x, nw, sc, sh, w_in, mu, w0, w_lora_b, a0, a_lora_b, g_lora_b, tm=256):
    b, s, d = x.shape
    nl = [w_lora_b.shape[0], a_lora_b.shape[0], g_lora_b.shape[0]]
    nlp = [_round_up(n, LANES) for n in nl]
    offs = [0, d, 2 * d, 3 * d, 3 * d + nl[0], 3 * d + nl[0] + nl[1]]
    wr, wk, wv = (w_in[:, offs[j]:offs[j] + d].astype(BF16) for j in range(3))
    wl = [_pad_cols(w_in[:, offs[3 + j]:offs[3 + j] + nl[j]], nlp[j]).astype(BF16) for j in range(3)]
    lb = [_pad_rows(m, n).astype(BF16) for m, n in zip((w_lora_b, a_lora_b, g_lora_b), nlp)]
    rowv = lambda a: a.reshape(1, d).astype(F32)
    tok = pl.BlockSpec((None, tm, d), lambda bi, i: (bi, i, 0))
    vec = pl.BlockSpec((None, 1, d), lambda bi, i: (bi, 0, 0))
    full = lambda a: pl.BlockSpec(a.shape, lambda bi, i: (0, 0))
    ins = [x, rowv(nw), sc, sh, mu.astype(F32), wr, wk, wv, wl[0], wl[1], wl[2],
           rowv(w0), lb[0], rowv(a0), lb[1], lb[2]]
    specs = [tok, full(ins[1]), vec, vec] + [full(a) for a in ins[4:]]
    dts = [BF16, BF16, BF16, F32, BF16, BF16]
    return pl.pallas_call(
        _rwkv_inproj_kernel,
        grid=(b, s // tm),
        in_specs=specs,
        out_specs=[tok] * 6,
        out_shape=[jax.ShapeDtypeStruct((b, s, d), dt) for dt in dts],
        scratch_shapes=[pltpu.VMEM((8, d), F32)],
        compiler_params=_cparams("parallel", "arbitrary"),
        name="rwkv_inproj",
    )(*ins)


RW_GN_EPS = 64e-5


def _rwkv_kernel(r_ref, k_ref, v_ref, lw_ref, a_ref, g_ref, kk_ref, ka_ref, rk_ref, lnw_ref, lnb_ref,
                 y_ref, s_scr, *, hd):
    i = pl.program_id(1)
    l, d = r_ref.shape
    pair = 2 * hd
    npairs = d // pair

    @pl.when(i == 0)
    def _():
        s_scr[...] = jnp.zeros_like(s_scr)

    lw = lw_ref[...]
    cum = _cumsum_rows(lw)
    lo1 = lax.broadcasted_iota(jnp.int32, (1, pair), 1) < hd
    r2 = lax.broadcasted_iota(jnp.int32, (2 * l, 1), 0) < l
    c2 = lax.broadcasted_iota(jnp.int32, (1, pair), 1) < hd
    stack_mask = r2 == c2
    rr = lax.broadcasted_iota(jnp.int32, (2 * l, 2 * l), 0)
    cc = lax.broadcasted_iota(jnp.int32, (2 * l, 2 * l), 1)
    same = (rr < l) == (cc < l)
    strict = same & (rr > cc)
    incl = same & (rr >= cc)
    kr = lax.broadcasted_iota(jnp.int32, (pair, pair), 0) < hd
    kc = lax.broadcasted_iota(jnp.int32, (pair, pair), 1) < hd
    bd = kr == kc

    def half_sum(t):
        s_lo = jnp.sum(jnp.where(lo1, t, 0.0), axis=-1, keepdims=True)
        s_hi = jnp.sum(jnp.where(lo1, 0.0, t), axis=-1, keepdims=True)
        return jnp.where(lo1, s_lo, s_hi)

    def stack(t, masked):
        t2 = jnp.concatenate([t, t], axis=0)
        return jnp.where(stack_mask, t2, 0.0) if masked else t2

    nt = (((1,), (1,)), ((), ()))
    tn = (((0,), (0,)), ((), ()))
    pairs = range(npairs)
    sls = [slice(p * pair, (p + 1) * pair) for p in pairs]
    st = []
    for p in pairs:
        sl = sls[p]
        r = r_ref[:, sl].astype(F32)
        k = k_ref[:, sl].astype(F32)
        v = v_ref[:, sl].astype(F32)
        a = a_ref[:, sl].astype(F32)
        cm = cum[:, sl]
        gam = jnp.exp(cm)
        gam_prev = jnp.exp(cm - lw[:, sl])
        inv_gam = jnp.exp(-cm)
        kk = k * kk_ref[:, sl]
        kk = kk / jnp.maximum(jnp.sqrt(half_sum(kk * kk)), 1e-12)
        k2 = k * (1.0 + (a - 1.0) * ka_ref[:, sl])
        at = stack(-kk * gam_prev, True).astype(BF16)
        rt = stack(r * gam, True).astype(BF16)
        bt = (kk * a * inv_gam).astype(BF16)
        kt = (k2 * inv_gam).astype(BF16)
        v_st = stack(v, False)
        s0 = s_scr[p]
        lhs = jnp.concatenate([at, rt], axis=0)
        bk = jnp.concatenate([bt, bt, kt, kt], axis=0)
        big = lax.dot_general(lhs, bk, nt, preferred_element_type=F32)
        ls = lax.dot_general(lhs, s0.astype(BF16), nt, preferred_element_type=F32)
        st.append(dict(
            v_st=v_st, v_b=v_st.astype(BF16), bk=bk, s0=s0, gam_last=gam[l - 1:l, :],
            bonus=half_sum(r * k2 * rk_ref[:, sl]) * v,
            m=jnp.where(strict, big[0:2 * l, 0:2 * l], 0.0).astype(BF16),
            a_ak=jnp.where(strict, big[0:2 * l, 2 * l:4 * l], 0.0).astype(BF16),
            a_r=jnp.concatenate([jnp.where(incl, big[2 * l:4 * l, 0:2 * l], 0.0),
                                 jnp.where(incl, big[2 * l:4 * l, 2 * l:4 * l], 0.0)],
                                axis=1).astype(BF16),
            ls_a=ls[0:2 * l], ls_r=ls[2 * l:4 * l]))

    for q in st:
        q["x"] = q["ls_a"] + jnp.dot(q["a_ak"], q["v_b"], preferred_element_type=F32)
    span = 1
    while span < l:
        for q in st:
            q["x"] = q["x"] + jnp.dot(q["m"], q["x"].astype(BF16), preferred_element_type=F32)
        span *= 2
        if span < l:
            for q in st:
                q["m"] = jnp.dot(q["m"], q["m"], preferred_element_type=F32).astype(BF16)

    for p in pairs:
        q, sl = st[p], sls[p]
        x_sol = q["x"]
        y_st = q["ls_r"] + jnp.dot(q["a_r"], jnp.concatenate([x_sol.astype(BF16), q["v_b"]], axis=0),
                                   preferred_element_type=F32)
        uv = jnp.concatenate([jnp.where(stack_mask, x_sol, 0.0),
                              jnp.where(stack_mask, q["v_st"], 0.0)], axis=0).astype(BF16)
        upd = lax.dot_general(uv, q["bk"], tn, preferred_element_type=F32)
        s_scr[p] = (q["s0"] + jnp.where(bd, upd, 0.0)) * q["gam_last"]

        y = jnp.where(lo1, y_st[0:l], y_st[l:2 * l])
        mean = half_sum(y) * (1.0 / hd)
        yc = y - mean
        var = half_sum(yc * yc) * (1.0 / hd)
        y = yc * lax.rsqrt(var + RW_GN_EPS) * lnw_ref[:, sl] + lnb_ref[:, sl]
        y_ref[:, sl] = ((y + q["bonus"]) * g_ref[:, sl].astype(F32)).astype(y_ref.dtype)


def _rwkv_core(r, k, v, lw, a, g, k_k, k_a, r_k, ln_w, ln_b, hd, chunk=64):
    b, s, d = r.shape
    rowv = lambda t: t.reshape(1, d).astype(F32)
    tok = pl.BlockSpec((None, chunk, d), lambda bi, i: (bi, i, 0))
    full = pl.BlockSpec((1, d), lambda bi, i: (0, 0))
    return pl.pallas_call(
        functools.partial(_rwkv_kernel, hd=hd),
        grid=(b, s // chunk),
        in_specs=[tok] * 6 + [full] * 5,
        out_specs=tok,
        out_shape=jax.ShapeDtypeStruct((b, s, d), BF16),
        scratch_shapes=[pltpu.VMEM((d // (2 * hd), 2 * hd, 2 * hd), F32)],
        compiler_params=_cparams("parallel", "arbitrary"),
        name="rwkv_core",
    )(r, k, v, lw, a, g, rowv(k_k), rowv(k_a), rowv(r_k), rowv(ln_w), rowv(ln_b))


def _rwkv_layer(x, nw, sc, sh, g_mod, w_in, mu, w0, w_lora_b, a0, a_lora_b, g_lora_b, k_k, k_a, r_k,
                ln_w, ln_b, w_out, hd):
    r, k, v, lw, a, g = _rwkv_inproj(x, nw, sc, sh, w_in, mu, w0, w_lora_b, a0, a_lora_b, g_lora_b)
    y = _rwkv_core(r, k, v, lw, a, g, k_k, k_a, r_k, ln_w, ln_b, hd)
    return _outproj_residual(x, y, g_mod, w_out.astype(BF16))


MOE_TILE = 1024
MOE_ROWS = 128
MOE_SPECIALISED_ROWS = (192, 256, 320, 384)
MOE_GROUP = 2


def _router_kernel(x_ref, nw_ref, sc_ref, sh_ref, wr_ref, hb_ref, comb_ref, pos_ref, cnt_ref, *, ne):
    t = x_ref.shape[0]
    h = _norm_mod(x_ref[...], nw_ref[...], sc_ref[...], sh_ref[...])
    hb_ref[...] = h.astype(hb_ref.dtype)
    lane = lax.broadcasted_iota(jnp.int32, (t, LANES), 1)
    logits = jnp.where(lane < ne, _dot3(h, wr_ref[...]), NEG_BIG)
    m1 = jnp.max(logits, axis=-1, keepdims=True)
    i1 = jnp.min(jnp.where(logits == m1, lane, LANES), axis=-1, keepdims=True)
    sel1 = lane == i1
    rest = jnp.where(sel1, NEG_BIG, logits)
    m2 = jnp.max(rest, axis=-1, keepdims=True)
    i2 = jnp.min(jnp.where(rest == m2, lane, LANES), axis=-1, keepdims=True)
    sel2 = lane == i2
    e2 = jnp.exp(m2 - m1)
    w1 = 1.0 / (1.0 + e2)
    comb = jnp.where(sel1, w1, 0.0) + jnp.where(sel2, e2 * w1, 0.0)
    sel = jnp.where(sel1 | sel2, 1.0, 0.0)
    below = jnp.where(_tri(t, "strict_lower"), 1.0, 0.0).astype(BF16)
    slot = jnp.dot(below, sel.astype(BF16), preferred_element_type=F32)
    comb_ref[...] = comb.T
    pos_ref[...] = jnp.where(sel > 0.0, slot, -1.0).T
    cnt = jnp.sum(sel, axis=0, keepdims=True).astype(jnp.int32)
    cnt_ref[...] = jnp.broadcast_to(cnt, cnt_ref.shape)


def _moe_router(x, nw, sc, sh, w_router, tile):
    b, s, d = x.shape
    ne = w_router.shape[1]
    nt = s // tile
    vec = pl.BlockSpec((None, 1, d), lambda bi, i: (bi, 0, 0))
    return pl.pallas_call(
        functools.partial(_router_kernel, ne=ne),
        grid=(b, nt),
        in_specs=[pl.BlockSpec((None, tile, d), lambda bi, i: (bi, i, 0)),
                  pl.BlockSpec((1, d), lambda bi, i: (0, 0)), vec, vec,
                  pl.BlockSpec((d, LANES), lambda bi, i: (0, 0))],
        out_specs=[pl.BlockSpec((None, tile, d), lambda bi, i: (bi, i, 0)),
                   pl.BlockSpec((None, LANES, tile), lambda bi, i: (bi, 0, i)),
                   pl.BlockSpec((None, LANES, tile), lambda bi, i: (bi, 0, i)),
                   pl.BlockSpec((None, None, 8, LANES), lambda bi, i: (bi, i, 0, 0))],
        out_shape=[jax.ShapeDtypeStruct((b, s, d), BF16),
                   jax.ShapeDtypeStruct((b, LANES, s), F32),
                   jax.ShapeDtypeStruct((b, LANES, s), F32),
                   jax.ShapeDtypeStruct((b, nt, 8, LANES), jnp.int32)],
        compiler_params=_cparams("parallel", "parallel"),
        name="moe_router",
    )(x, nw.reshape(1, d), sc, sh, _pad_cols(w_router, LANES))


def _expert_kernel(cnt_ref, x_ref, hb_ref, pos_ref, comb_ref, g_ref, wg_ref, wu_ref, wd_ref, *rest,
                   rows, ne, tile, nsub, final_norm):
    fnw_ref = rest[0] if final_norm else None
    o_ref, hg_scr, yacc_scr = rest[-3:]
    bi, gi, e, f = (pl.program_id(a) for a in range(4))
    nf = pl.num_programs(3)
    ntiles = pl.num_programs(1) * nsub

    def run_tile(sub, carry):
        cnt = cnt_ref[(bi * ntiles + gi * nsub + sub) * ne + e]
        nblk = lax.shift_right_logical(cnt + (rows - 1), int(math.log2(rows)))
        tok = pl.ds(pl.multiple_of(sub * tile, tile), tile)

        @pl.when((e == 0) & (f == 0))
        def _():
            o_ref[tok, :] = x_ref[tok, :]

        def run_rows(row0, m):
            rs = pl.ds(row0, m)

            def one_hot():
                slot = pos_ref[sub, pl.ds(e, 1), :]
                want = (lax.broadcasted_iota(jnp.int32, (m, 1), 0) + row0).astype(F32)
                return slot == want

            @pl.when(f == 0)
            def _():
                p = jnp.where(one_hot(), 1.0, 0.0).astype(BF16)
                hg_scr[sub, rs, :] = jnp.dot(p, hb_ref[tok, :],
                                             preferred_element_type=F32).astype(BF16)

            hg = hg_scr[sub, rs, :]
            gate = jnp.dot(hg, wg_ref[...], preferred_element_type=F32)
            up = jnp.dot(hg, wu_ref[...], preferred_element_type=F32)
            act = (gate * _sigmoid(gate) * up).astype(BF16)
            part = jnp.dot(act, wd_ref[...], preferred_element_type=F32)

            @pl.when(f == 0)
            def _():
                yacc_scr[sub, rs, :] = part

            @pl.when(f > 0)
            def _():
                yacc_scr[sub, rs, :] += part

            @pl.when(f == nf - 1)
            def _():
                hit = one_hot()
                wrow = comb_ref[sub, pl.ds(e, 1), :]
                wgt = jnp.sum(jnp.where(hit, wrow, 0.0), axis=-1, keepdims=True)
                yw = (yacc_scr[sub, rs, :] * wgt * g_ref[...]).astype(BF16)
                p = jnp.where(hit, 1.0, 0.0).astype(BF16)
                o_ref[tok, :] += lax.dot_general(p, yw, (((0,), (0,)), ((), ())),
                                                 preferred_element_type=F32)

        lo = 0
        for m in MOE_SPECIALISED_ROWS:
            @pl.when((cnt > lo) & (cnt <= m))
            def _():
                run_rows(0, m)
            lo = m

        @pl.when(cnt > lo)
        def _():
            def body(j, c):
                run_rows(pl.multiple_of(j * rows, rows), rows)
                return c
            lax.fori_loop(0, nblk, body, 0)

        if final_norm:
            @pl.when((e == ne - 1) & (f == nf - 1))
            def _():
                xo = o_ref[tok, :]
                ms = jnp.mean(xo * xo, axis=-1, keepdims=True)
                o_ref[tok, :] = xo * lax.rsqrt(ms + RMS_EPS) * fnw_ref[...]

        return carry

    lax.fori_loop(0, nsub, run_tile, 0)


def _moe_experts(x, hb, pos, comb, counts, g, w_gu_bf16, w_down_bf16, layer, tile, nsub, rows,
                 final_norm_w=None, tf=896):
    b, s, d = x.shape
    final_norm = final_norm_w is not None
    extra_specs = [pl.BlockSpec((1, d), lambda bi, i, e, f, c: (0, 0))] if final_norm else []
    extra_args = [final_norm_w.reshape(1, d).astype(F32)] if final_norm else []
    _, ne, dff, _ = w_down_bf16.shape
    nf = dff // tf
    grp = tile * nsub
    once = pl.Buffered(1)
    grid_spec = pltpu.PrefetchScalarGridSpec(
        num_scalar_prefetch=1,
        grid=(b, s // grp, ne, nf),
        in_specs=[
            pl.BlockSpec((None, grp, d), lambda bi, i, e, f, c: (bi, i, 0), pipeline_mode=once),
            pl.BlockSpec((None, grp, d), lambda bi, i, e, f, c: (bi, i, 0), pipeline_mode=once),
            pl.BlockSpec((None, nsub, 8, tile), lambda bi, i, e, f, c: (bi, i, 0, 0)),
            pl.BlockSpec((None, nsub, 8, tile), lambda bi, i, e, f, c: (bi, i, 0, 0)),
            pl.BlockSpec((None, 1, d), lambda bi, i, e, f, c: (bi, 0, 0)),
            pl.BlockSpec((None, None, d, tf), lambda bi, i, e, f, c: (layer, e, 0, f)),
            pl.BlockSpec((None, None, d, tf), lambda bi, i, e, f, c: (layer, e, 0, f + nf)),
            pl.BlockSpec((None, None, tf, d), lambda bi, i, e, f, c: (layer, e, f, 0)),
        ] + extra_specs,
        out_specs=pl.BlockSpec((None, grp, d), lambda bi, i, e, f, c: (bi, i, 0)),
        scratch_shapes=[pltpu.VMEM((nsub, tile, d), BF16), pltpu.VMEM((nsub, tile, d), F32)],
    )
    return pl.pallas_call(
        functools.partial(_expert_kernel, rows=rows, ne=ne, tile=tile, nsub=nsub,
                          final_norm=final_norm),
        grid_spec=grid_spec,
        out_shape=jax.ShapeDtypeStruct((b, s, d), F32),
        input_output_aliases={1: 0},
        compiler_params=_cparams("parallel", "parallel", "arbitrary", "arbitrary"),
        name="moe_experts",
    )(counts, x, hb, pos, comb, g, w_gu_bf16, w_gu_bf16, w_down_bf16, *extra_args)


def _moe_layer(x, nw, sc, sh, g, w_router, w_gu_bf16, w_down_bf16, layer, final_norm_w=None):
    b, s, _ = x.shape
    ne = w_router.shape[1]
    tile = min(MOE_TILE, s)
    nsub = min(MOE_GROUP, s // tile)
    hb, comb_t, pos_t, counts = _moe_router(x, nw, sc, sh, w_router, tile)
    counts = counts[:, :, 0, :ne].reshape(-1)
    by_tile = lambda a: jnp.swapaxes(a[:, :8].reshape(b, 8, s // tile, tile), 1, 2)
    return _moe_experts(x, hb, by_tile(pos_t), by_tile(comb_t), counts, g, w_gu_bf16, w_down_bf16,
                        layer, tile, nsub, MOE_ROWS, final_norm_w)


def _final_norm_kernel(x_ref, w_ref, o_ref):
    x = x_ref[...]
    ms = jnp.mean(x * x, axis=-1, keepdims=True)
    o_ref[...] = x * lax.rsqrt(ms + RMS_EPS) * w_ref[...]


def _final_norm(x, w, tm=1024):
    b, s, d = x.shape
    return pl.pallas_call(
        _final_norm_kernel,
        grid=(b, s // tm),
        in_specs=[pl.BlockSpec((None, tm, d), lambda bi, i: (bi, i, 0)),
                  pl.BlockSpec((1, d), lambda bi, i: (0, 0))],
        out_specs=pl.BlockSpec((None, tm, d), lambda bi, i: (bi, i, 0)),
        out_shape=jax.ShapeDtypeStruct((b, s, d), F32),
        compiler_params=_cparams("parallel", "parallel"),
        name="final_norm",
    )(x, w.reshape(1, d))


ML_HEADS = 4
SSD_GROUPS, SSD_HEADDIM, SSD_STATE = 4, 64, 128
RW_HEADDIM = 64


def kernel(x, c, ada_w, ada_b, norm_w, final_norm_w, ml_w_in, ml_gate_b, ml_norm_w, ml_w_out, ssd_w_in, ssd_conv_w, ssd_conv_b, ssd_dt_bias, ssd_a_log, ssd_d, ssd_norm_w, ssd_w_out, rw_w_in, rw_mu, rw_w0, rw_w_lora_b, rw_a0, rw_a_lora_b, rw_g_lora_b, rw_k_k, rw_k_a, rw_r_k, rw_ln_w, rw_ln_b, rw_w_out, lru_w_in, lru_conv_w, lru_conv_b, lru_gx_w, lru_gx_b, lru_ga_w, lru_ga_b, lru_lambda, lru_w_out, ffn_w_gu, ffn_w_down, moe_router, moe_w_gu, moe_w_down):
    depth = ada_w.shape[0]
    mod = _ada_modulation(c, ada_w, ada_b)
    moe_gu_b, moe_down_b = moe_w_gu.astype(BF16), moe_w_down.astype(BF16)
    for layer in range(depth):
        sh_t, sc_t, g_t, sh_c, sc_c, g_c = [m[:, None, :] for m in jnp.split(mod[layer], 6, axis=-1)]
        kind, j = layer % 4, layer // 4
        nw_t, nw_c = norm_w[layer, 0], norm_w[layer, 1]
        if kind == 0:
            x = _mlstm_layer(x, nw_t, sc_t, sh_t, g_t, ml_w_in[j], ml_gate_b[j], ml_norm_w[j],
                             ml_w_out[j], ML_HEADS)
        elif kind == 1:
            x = _ssd_layer(x, nw_t, sc_t, sh_t, g_t, ssd_w_in[j], ssd_conv_w[j], ssd_conv_b[j],
                           ssd_dt_bias[j], ssd_a_log[j], ssd_d[j], ssd_norm_w[j], ssd_w_out[j],
                           SSD_GROUPS, SSD_HEADDIM, SSD_STATE)
        elif kind == 2:
            x = _rwkv_layer(x, nw_t, sc_t, sh_t, g_t, rw_w_in[j], rw_mu[j], rw_w0[j], rw_w_lora_b[j],
                            rw_a0[j], rw_a_lora_b[j], rw_g_lora_b[j], rw_k_k[j], rw_k_a[j], rw_r_k[j],
                            rw_ln_w[j], rw_ln_b[j], rw_w_out[j], RW_HEADDIM)
        else:
            x = _rglru_layer(x, nw_t, sc_t, sh_t, g_t, lru_w_in[j], lru_conv_w[j], lru_conv_b[j],
                             lru_gx_w[j], lru_gx_b[j], lru_ga_w[j], lru_ga_b[j], lru_lambda[j],
                             lru_w_out[j])
        if layer % 2 == 0:
            x = _ffn_dense(x, nw_c, sc_c, sh_c, g_c, ffn_w_gu[layer // 2].astype(BF16),
                           ffn_w_down[layer // 2].astype(BF16))
        else:
            closing = final_norm_w if layer == depth - 1 else None
            x = _moe_layer(x, nw_c, sc_c, sh_c, g_c, moe_router[layer // 2], moe_gu_b, moe_down_b,
                           layer // 2, closing)
    if depth % 2 == 0:
        return x
    return _final_norm(x, final_norm_w)
```
